```python
import math
import jax
import jax.numpy as jnp
from jax import lax
import numpy as np

D_MODEL = 1024
BATCH = 4
SEQ = 4096
DEPTH = 4
DEC_BATCH = 128
DEC_SEQ = 4
PAST_LEN = 2048
PAGE_SIZE = 128

HEAD_DIM = 128
N_MIXERS = 2
DIL_GROUPS = ((128, 1), (512, 4), (2048, 16))
N_GROUPS = len(DIL_GROUPS)
HEADS_PER_GROUP = 4
N_HEADS_A = N_GROUPS * HEADS_PER_GROUP
WIDTH_A_QKV = N_HEADS_A * HEAD_DIM
WIDTH_A_OUT = HEADS_PER_GROUP * HEAD_DIM
N_MEM = 256
MEM_HEADS = 4
WIDTH_MEM = MEM_HEADS * HEAD_DIM
N_HEADS_B = 8
WIDTH_B = N_HEADS_B * HEAD_DIM
CONV_W = 4
CHUNK = 64
BRANCH_A = WIDTH_A_OUT + WIDTH_MEM
BRANCH_B = WIDTH_B + WIDTH_MEM
IN_A = 3 * WIDTH_A_QKV + WIDTH_MEM + BRANCH_A
IN_B = 3 * WIDTH_B + 2 * N_HEADS_B + WIDTH_MEM + BRANCH_B
N_LAYERS_A = (DEPTH + 1) // 2
N_LAYERS_B = DEPTH // 2
N_BUCKETS = 32
MAX_EXACT = N_BUCKETS // 2
MAX_DIST = 2048
EPS = 1e-6
NEG = -1e30
SCALE = HEAD_DIM ** -0.5

kernel_name = 'dilated_swa_gated_delta_hybrid_step'


def rmsnorm(x, w):
    xf = x.astype(jnp.float32)
    y = xf * lax.rsqrt(jnp.mean(xf * xf, axis=-1, keepdims=True) + EPS)
    return (y * w.astype(jnp.float32)).astype(x.dtype)


def l2norm(x):
    return x * lax.rsqrt(jnp.sum(x * x, axis=-1, keepdims=True) + EPS)


def rel_bucket(dist):
    df = jnp.maximum(dist, 1).astype(jnp.float32)
    large = MAX_EXACT + (jnp.log(df / MAX_EXACT) / math.log(MAX_DIST / MAX_EXACT)
                         * (N_BUCKETS - MAX_EXACT)).astype(jnp.int32)
    return jnp.where(dist < MAX_EXACT, dist, jnp.minimum(large, N_BUCKETS - 1))


def mem_attend(q, mk, mv):
    s = jnp.einsum('bthe,bmhe->bhtm', q, mk).astype(jnp.float32) * SCALE
    p = jax.nn.softmax(s, axis=-1).astype(mv.dtype)
    return jnp.einsum('bhtm,bmhe->bthe', p, mv)


def dilated_group_prompt(q, k, v, dil, win, bias_tab):
    B, T, H, hd = q.shape
    nb = win // dil
    span = nb * dil
    Tp = -(-T // span) * span
    nblk = Tp // span
    pad = ((0, 0), (0, Tp - T), (0, 0), (0, 0))
    qb, kb, vb = [jnp.pad(a, pad).reshape(B, nblk, nb, dil, H, hd) for a in (q, k, v)]

    def with_prev(a):
        prev = jnp.pad(a, ((0, 0), (1, 0), (0, 0), (0, 0), (0, 0), (0, 0)))[:, :-1]
        return jnp.concatenate([prev, a], axis=2)

    kk, vv = with_prev(kb), with_prev(vb)
    i = jnp.arange(nb)[:, None]
    j = jnp.arange(2 * nb)[None, :]
    dsub = i + nb - j
    band = (dsub >= 0) & (dsub <= nb)
    blk = jnp.arange(nblk)[:, None, None]
    valid = band[None] & ((blk * nb + j[None] - nb) >= 0)
    bias = bias_tab[rel_bucket(jnp.maximum(dsub, 0) * dil)].astype(jnp.float32)
    s = jnp.einsum('bnirhe,bnjrhe->bnrhij', qb, kk).astype(jnp.float32) * SCALE
    s = s + bias.transpose(2, 0, 1)
    s = jnp.where(valid[None, :, None, None], s, NEG)
    lse = jax.nn.logsumexp(s, axis=-1)
    p = jnp.exp(s - lse[..., None]).astype(v.dtype)
    o = jnp.einsum('bnrhij,bnjrhe->bnirhe', p, vv).reshape(B, Tp, H, hd)[:, :T]
    lse = lse.transpose(0, 1, 4, 2, 3).reshape(B, Tp, H)[:, :T]
    return o, lse


def dilated_group_sample(q, k, v, kbuf, vbuf, dil, win, bias_tab):
    B, T, H, hd = q.shape
    L = kbuf.shape[1]
    kc = jnp.concatenate([kbuf, k], axis=1)
    vc = jnp.concatenate([vbuf, v], axis=1)
    nk = win // dil + 1
    jd = jnp.arange(nk) * dil
    pos = PAST_LEN + jnp.arange(T)[:, None] - jd[None]
    idx = pos - (PAST_LEN - L)
    valid = (pos >= 0) & (idx >= 0)
    idx = jnp.clip(idx, 0, L + T - 1)
    kg = kc[:, idx]
    vg = vc[:, idx]
    bias = bias_tab[rel_bucket(jd)].astype(jnp.float32)
    s = jnp.einsum('bthe,btjhe->bthj', q, kg).astype(jnp.float32) * SCALE + bias.T
    s = jnp.where(valid[None, :, None, :], s, NEG)
    lse = jax.nn.logsumexp(s, axis=-1)
    p = jnp.exp(s - lse[..., None]).astype(v.dtype)
    o = jnp.einsum('bthj,btjhe->bthe', p, vg)
    keep = min(win, L + T)
    return o, lse, kc[:, L + T - keep:], vc[:, L + T - keep:]


def layer_a(h, mem_k, mem_v, w_in, w_out, rel_bias, bufs):
    B, T, _ = h.shape
    proj = h @ w_in
    q, k, v, qm, z = jnp.split(proj, [WIDTH_A_QKV, 2 * WIDTH_A_QKV, 3 * WIDTH_A_QKV,
                                      3 * WIDTH_A_QKV + WIDTH_MEM], axis=-1)
    q, k, v = [a.reshape(B, T, N_GROUPS, HEADS_PER_GROUP, HEAD_DIM) for a in (q, k, v)]
    outs, lses, new_bufs = [], [], []
    for g, (win, dil) in enumerate(DIL_GROUPS):
        bias_g = rel_bias[:, g * HEADS_PER_GROUP:(g + 1) * HEADS_PER_GROUP]
        if bufs is None:
            o, l = dilated_group_prompt(q[:, :, g], k[:, :, g], v[:, :, g], dil, win, bias_g)
            keep = min(win, T)
            nkb, nvb = k[:, T - keep:, g], v[:, T - keep:, g]
        else:
            o, l, nkb, nvb = dilated_group_sample(q[:, :, g], k[:, :, g], v[:, :, g],
                                                  bufs[2 * g], bufs[2 * g + 1], dil, win, bias_g)
        outs.append(o)
        lses.append(l)
        new_bufs.extend([nkb, nvb])
    alpha = jax.nn.softmax(jnp.stack(lses, axis=0), axis=0)
    o = jnp.einsum('gbth,gbthe->bthe', alpha.astype(h.dtype), jnp.stack(outs, axis=0))
    om = mem_attend(qm.reshape(B, T, MEM_HEADS, HEAD_DIM), mem_k, mem_v)
    branch = jnp.concatenate([o.reshape(B, T, WIDTH_A_OUT), om.reshape(B, T, WIDTH_MEM)], axis=-1)
    return (branch * jax.nn.silu(z)) @ w_out, new_bufs


def to_chunks(a, C, N):
    B, T = a.shape[:2]
    a = jnp.pad(a, [(0, 0), (0, N * C - T)] + [(0, 0)] * (a.ndim - 2))
    a = a.reshape((B, N, C) + a.shape[2:])
    return a.transpose((1, 0, 3, 2) + tuple(range(4, a.ndim)))


def gated_delta(q, k, v, g, beta, S0):
    B, T, H, dk = q.shape
    dv = v.shape[-1]
    C = min(CHUNK, T)
    N = -(-T // C)
    qc, kc, vc = to_chunks(q, C, N), to_chunks(k, C, N), to_chunks(v, C, N)
    gc = jnp.cumsum(to_chunks(g, C, N), axis=-1)
    bc = to_chunks(beta, C, N)
    tri = jnp.tril(jnp.ones((C, C), dtype=bool))
    strict = jnp.tril(jnp.ones((C, C), dtype=bool), -1)
    diff = gc[..., :, None] - gc[..., None, :]
    decay = jnp.where(tri, jnp.exp(jnp.where(tri, diff, 0.0)), 0.0)
    kb = kc * bc[..., None]
    lmat = jnp.where(strict, jnp.einsum('nbhik,nbhjk->nbhij', kb, kc) * decay, 0.0)
    rhs = jnp.concatenate([vc * bc[..., None], kb * jnp.exp(gc)[..., None]], axis=-1)
    sol = lax.linalg.triangular_solve(lmat, rhs, left_side=True, lower=True, unit_diagonal=True)
    u, w = sol[..., :dv], sol[..., dv:]
    aqk = jnp.einsum('nbhik,nbhjk->nbhij', qc, kc) * decay
    qg = qc * jnp.exp(gc)[..., None]
    g_last = gc[..., -1]
    kdec = kc * jnp.exp(g_last[..., None] - gc)[..., None]

    def step(S, xs):
        u_i, w_i, aqk_i, qg_i, kdec_i, gl_i = xs
        v_new = u_i - jnp.einsum('bhck,bhkv->bhcv', w_i, S)
        o_i = jnp.einsum('bhck,bhkv->bhcv', qg_i, S) + jnp.einsum('bhij,bhjv->bhiv', aqk_i, v_new)
        S = S * jnp.exp(gl_i)[..., None, None] + jnp.einsum('bhck,bhcv->bhkv', kdec_i, v_new)
        return S, o_i

    S, o = lax.scan(step, S0, (u, w, aqk, qg, kdec, g_last))
    o = o.transpose(1, 0, 3, 2, 4).reshape(B, N * C, H, dv)[:, :T]
    return o, S


def layer_b(h, mem_k, mem_v, w_in, conv_w, a_log, dt_bias, o_norm_w, w_out, conv_state, delta_state):
    B, T, _ = h.shape
    proj = h @ w_in
    qkv, b, a, qm, z = jnp.split(proj, [3 * WIDTH_B, 3 * WIDTH_B + N_HEADS_B,
                                        3 * WIDTH_B + 2 * N_HEADS_B,
                                        3 * WIDTH_B + 2 * N_HEADS_B + WIDTH_MEM], axis=-1)
    if conv_state is None:
        prefix = jnp.zeros((B, CONV_W - 1, 3 * WIDTH_B), qkv.dtype)
    else:
        prefix = conv_state.astype(qkv.dtype)
    xpad = jnp.concatenate([prefix, qkv], axis=1)
    conv = xpad[:, 0:T] * conv_w[0]
    for wi in range(1, CONV_W):
        conv = conv + xpad[:, wi:wi + T] * conv_w[wi]
    new_conv = xpad[:, T:]
    conv = jax.nn.silu(conv).astype(jnp.float32)
    q, k, v = [c.reshape(B, T, N_HEADS_B, HEAD_DIM) for c in jnp.split(conv, 3, axis=-1)]
    q = l2norm(q) * SCALE
    k = l2norm(k)
    beta = jax.nn.sigmoid(b.astype(jnp.float32))
    g = -jnp.exp(a_log.astype(jnp.float32)) * jax.nn.softplus(a.astype(jnp.float32)
                                                              + dt_bias.astype(jnp.float32))
    if delta_state is None:
        S0 = jnp.zeros((B, N_HEADS_B, HEAD_DIM, HEAD_DIM), jnp.float32)
        sdt = h.dtype
    else:
        S0 = delta_state.astype(jnp.float32)
        sdt = delta_state.dtype
    o, S = gated_delta(q, k, v, g, beta, S0)
    o = rmsnorm(o, o_norm_w).astype(h.dtype).reshape(B, T, WIDTH_B)
    om = mem_attend(qm.reshape(B, T, MEM_HEADS, HEAD_DIM), mem_k, mem_v).reshape(B, T, WIDTH_MEM)
    branch = jnp.concatenate([o, om], axis=-1) * jax.nn.silu(z)
    return branch @ w_out, new_conv, S.astype(sdt)


def setup_inputs(seed: int = 0) -> dict:
    key = jax.random.key(seed)
    ks = iter(jax.random.split(key, 32))

    def nrm(shape, scale):
        return jax.random.normal(next(ks), shape, jnp.float32) * scale

    def win(g):
        return (N_LAYERS_A, DEC_BATCH, min(DIL_GROUPS[g][0], PAST_LEN), HEADS_PER_GROUP, HEAD_DIM)

    mem_shape = (DEPTH, DEC_BATCH, N_MEM, MEM_HEADS, HEAD_DIM)
    dt = jnp.exp(jax.random.uniform(next(ks), (N_LAYERS_B, N_HEADS_B), jnp.float32,
                                    math.log(1e-3), math.log(1e-1)))
    return {
        'x_prompt': nrm((BATCH, SEQ, D_MODEL), 1.0),
        'x_sample': nrm((DEC_BATCH, DEC_SEQ, D_MODEL), 1.0),
        'cache_win_k0': nrm(win(0), 1.0),
        'cache_win_v0': nrm(win(0), 1.0),
        'cache_win_k1': nrm(win(1), 1.0),
        'cache_win_v1': nrm(win(1), 1.0),
        'cache_win_k2': nrm(win(2), 1.0),
        'cache_win_v2': nrm(win(2), 1.0),
        'state_conv': nrm((N_LAYERS_B, DEC_BATCH, CONV_W - 1, 3 * WIDTH_B), 1.0),
        'state_delta': nrm((N_LAYERS_B, DEC_BATCH, N_HEADS_B, HEAD_DIM, HEAD_DIM), 0.1),
        'cache_mem_k': nrm(mem_shape, 1.0),
        'cache_mem_v': nrm(mem_shape, 1.0),
        'mem_prompt': nrm((BATCH, N_MEM, D_MODEL), 1.0),
        'norm_w': 1.0 + nrm((DEPTH, D_MODEL), 0.02),
        'final_norm_w': 1.0 + nrm((D_MODEL,), 0.02),
        'rel_bias': nrm((N_BUCKETS, N_HEADS_A), 0.5),
        'w_in_a': nrm((N_LAYERS_A, D_MODEL, IN_A), D_MODEL ** -0.5),
        'w_out_a': nrm((N_LAYERS_A, BRANCH_A, D_MODEL), BRANCH_A ** -0.5),
        'w_in_b': nrm((N_LAYERS_B, D_MODEL, IN_B), D_MODEL ** -0.5),
        'conv_w': nrm((N_LAYERS_B, CONV_W, 3 * WIDTH_B), CONV_W ** -0.5),
        'a_log': jnp.log(jax.random.uniform(next(ks), (N_LAYERS_B, N_HEADS_B), jnp.float32, 1.0, 16.0)),
        'dt_bias': jnp.log(jnp.expm1(dt)),
        'o_norm_w': 1.0 + nrm((N_LAYERS_B, HEAD_DIM), 0.02),
        'w_out_b': nrm((N_LAYERS_B, BRANCH_B, D_MODEL), BRANCH_B ** -0.5),
        'w_mem_kv': nrm((DEPTH, D_MODEL, 2 * WIDTH_MEM), D_MODEL ** -0.5),
    }


def reference(x_prompt, x_sample, cache_win_k0, cache_win_v0, cache_win_k1, cache_win_v1,
              cache_win_k2, cache_win_v2, state_conv, state_delta, cache_mem_k, cache_mem_v,
              mem_prompt, norm_w, final_norm_w, rel_bias, w_in_a, w_out_a, w_in_b, conv_w,
              a_log, dt_bias, o_norm_w, w_out_b, w_mem_kv):
    xp, xs = x_prompt, x_sample
    Bp = x_prompt.shape[0]
    p_win = [[] for _ in range(2 * N_GROUPS)]
    s_win = [[] for _ in range(2 * N_GROUPS)]
    p_conv, p_delta, s_conv, s_delta, p_mk, p_mv = [], [], [], [], [], []
    for i in range(DEPTH):
        li = i // N_MIXERS
        hp = rmsnorm(xp, norm_w[i])
        hs = rmsnorm(xs, norm_w[i])
        mkv = jnp.einsum('bmd,de->bme', mem_prompt, w_mem_kv[i])
        mk_p = mkv[..., :WIDTH_MEM].reshape(Bp, N_MEM, MEM_HEADS, HEAD_DIM)
        mv_p = mkv[..., WIDTH_MEM:].reshape(Bp, N_MEM, MEM_HEADS, HEAD_DIM)
        p_mk.append(mk_p)
        p_mv.append(mv_p)
        if i % N_MIXERS == 0:
            yp, bp = layer_a(hp, mk_p, mv_p, w_in_a[li], w_out_a[li], rel_bias, None)
            ys, bs = layer_a(hs, cache_mem_k[i], cache_mem_v[i], w_in_a[li], w_out_a[li], rel_bias,
                             (cache_win_k0[li], cache_win_v0[li], cache_win_k1[li],
                              cache_win_v1[li], cache_win_k2[li], cache_win_v2[li]))
            for n in range(2 * N_GROUPS):
                p_win[n].append(bp[n])
                s_win[n].append(bs[n])
        else:
            yp, cp, dp = layer_b(hp, mk_p, mv_p, w_in_b[li], conv_w[li], a_log[li], dt_bias[li],
                                 o_norm_w[li], w_out_b[li], None, None)
            ys, cs, ds = layer_b(hs, cache_mem_k[i], cache_mem_v[i], w_in_b[li], conv_w[li],
                                 a_log[li], dt_bias[li], o_norm_w[li], w_out_b[li],
                                 state_conv[li], state_delta[li])
            p_conv.append(cp)
            p_delta.append(dp)
            s_conv.append(cs)
            s_delta.append(ds)
        xp = xp + yp
        xs = xs + ys
    y_prompt = rmsnorm(xp, final_norm_w)
    y_sample = rmsnorm(xs, final_norm_w)
    return (y_prompt, y_sample,
            jnp.stack(p_win[0]), jnp.stack(p_win[1]), jnp.stack(p_win[2]),
            jnp.stack(p_win[3]), jnp.stack(p_win[4]), jnp.stack(p_win[5]),
            jnp.stack(p_conv), jnp.stack(p_delta), jnp.stack(p_mk), jnp.stack(p_mv),
            jnp.stack(s_win[0]), jnp.stack(s_win[1]), jnp.stack(s_win[2]),
            jnp.stack(s_win[3]), jnp.stack(s_win[4]), jnp.stack(s_win[5]),
            jnp.stack(s_conv), jnp.stack(s_delta))
```

```python
import functools
import math

import jax
import jax.numpy as jnp
import numpy as np
from jax import lax
from jax.experimental import pallas as pl
from jax.experimental.pallas import tpu as pltpu

F32 = jnp.float32
BF16 = jnp.bfloat16

D_MODEL = 1024
HEAD_DIM = 128
DIL_GROUPS = ((128, 1), (512, 4), (2048, 16))
N_GROUPS = len(DIL_GROUPS)
HEADS_PER_GROUP = 4
WIDTH_A_QKV = N_GROUPS * HEADS_PER_GROUP * HEAD_DIM
WIDTH_A_OUT = HEADS_PER_GROUP * HEAD_DIM
N_MEM = 256
MEM_HEADS = 4
WIDTH_MEM = MEM_HEADS * HEAD_DIM
N_HEADS_B = 8
WIDTH_B = N_HEADS_B * HEAD_DIM
CONV_W = 4
N_BUCKETS = 32
MAX_EXACT = N_BUCKETS // 2
MAX_DIST = 2048
EPS = 1e-6
NEG = -1e30
SCALE = HEAD_DIM ** -0.5
BAND = 128
BLOCK_ROWS = 128
IN_A = 3 * WIDTH_A_QKV + WIDTH_MEM + WIDTH_A_OUT + WIDTH_MEM
IN_B_PAD = 5376
BA_COL_BLOCK = (3 * WIDTH_B + WIDTH_B + WIDTH_MEM + WIDTH_MEM) // 128
VMEM_LIMIT = 48 * 1024 * 1024


def _cparams(sem):
    return pltpu.CompilerParams(dimension_semantics=sem, vmem_limit_bytes=VMEM_LIMIT)


def _sigmoid(x):
    return 1.0 / (1.0 + jnp.exp(-x))


def _silu(x):
    return x * _sigmoid(x)


def _softplus(x):
    return jnp.maximum(x, 0.0) + jnp.log(1.0 + jnp.exp(-jnp.abs(x)))


def _dot(a, b):
    return jnp.dot(a.astype(BF16), b.astype(BF16), preferred_element_type=F32)


def _dot_nt(a, b):
    return lax.dot_general(a.astype(BF16), b.astype(BF16), (((1,), (1,)), ((), ())),
                           preferred_element_type=F32)


def _split3(a):
    hi = a.astype(BF16)
    r1 = a - hi.astype(F32)
    mid = r1.astype(BF16)
    lo = (r1 - mid.astype(F32)).astype(BF16)
    return hi, mid, lo


def _dot_precise(a, b):
    ah, am, al = _split3(a)
    bh, bm, bl = _split3(b)
    d = lambda x, y: jnp.dot(x, y, preferred_element_type=F32)
    return (d(ah, bh) + (d(ah, bm) + d(am, bh))) + ((d(am, bm) + d(ah, bl)) + d(al, bh))


def _norm_proj_kernel(x_ref, nw_ref, w_ref, o_ref, h_ref):
    @pl.when(pl.program_id(1) == 0)
    def _():
        x = x_ref[...]
        ms = jnp.mean(x * x, axis=-1, keepdims=True)
        h_ref[...] = (x * lax.rsqrt(ms + EPS) * nw_ref[...]).astype(BF16)

    o_ref[...] = jnp.dot(h_ref[...], w_ref[...], preferred_element_type=F32)


def _norm_proj(x, nw, w, tm, tn):
    m, d = x.shape
    n = w.shape[1]
    return pl.pallas_call(
        _norm_proj_kernel,
        grid=(m // tm, n // tn),
        in_specs=[pl.BlockSpec((tm, d), lambda i, j: (i, 0)),
                  pl.BlockSpec((1, d), lambda i, j: (0, 0)),
                  pl.BlockSpec((d, tn), lambda i, j: (0, j))],
        out_specs=pl.BlockSpec((tm, tn), lambda i, j: (i, j)),
        out_shape=jax.ShapeDtypeStruct((m, n), F32),
        scratch_shapes=[pltpu.VMEM((tm, d), BF16)],
        compiler_params=_cparams(("parallel", "arbitrary")),
        name="norm_proj",
    )(x, nw, w)


def _final_norm_kernel(x_ref, nw_ref, o_ref):
    x = x_ref[...]
    ms = jnp.mean(x * x, axis=-1, keepdims=True)
    o_ref[...] = x * lax.rsqrt(ms + EPS) * nw_ref[...]


def _final_norm(x, nw, tm):
    m, d = x.shape
    return pl.pallas_call(
        _final_norm_kernel,
        grid=(m // tm,),
        in_specs=[pl.BlockSpec((tm, d), lambda i: (i, 0)),
                  pl.BlockSpec((1, d), lambda i: (0, 0))],
        out_specs=pl.BlockSpec((tm, d), lambda i: (i, 0)),
        out_shape=jax.ShapeDtypeStruct((m, d), F32),
        compiler_params=_cparams(("parallel",)),
        name="final_norm",
    )(x, nw)


def _mem_kv_kernel(x_ref, w_ref, k_ref, v_ref):
    r = jnp.dot(x_ref[...].astype(BF16), w_ref[...], preferred_element_type=F32)
    k_ref[...] = r[:, :WIDTH_MEM]
    v_ref[...] = r[:, WIDTH_MEM:]


def _mem_kv(mem, w):
    m, d = mem.shape
    depth = w.shape[0]
    out = jax.ShapeDtypeStruct((depth, m, WIDTH_MEM), F32)
    return pl.pallas_call(
        _mem_kv_kernel,
        grid=(depth,),
        in_specs=[pl.BlockSpec((m, d), lambda i: (0, 0)),
                  pl.BlockSpec((None, d, 2 * WIDTH_MEM), lambda i: (i, 0, 0))],
        out_specs=[pl.BlockSpec((None, m, WIDTH_MEM), lambda i: (i, 0, 0)),
                   pl.BlockSpec((None, m, WIDTH_MEM), lambda i: (i, 0, 0))],
        out_shape=[out, out],
        compiler_params=_cparams(("parallel",)),
        name="mem_kv",
    )(mem, w)


def _rel_bucket_np(dist):
    dist = np.asarray(dist, np.int64)
    df = np.maximum(dist, 1).astype(np.float32)
    large = MAX_EXACT + (np.log(df / np.float32(MAX_EXACT)) / np.float32(math.log(MAX_DIST / MAX_EXACT))
                         * np.float32(N_BUCKETS - MAX_EXACT)).astype(np.int32)
    return np.where(dist < MAX_EXACT, dist, np.minimum(large, N_BUCKETS - 1)).astype(np.int32)


def _bias_prompt_kernel(bkt_ref, tab_ref, o_ref):
    g = pl.program_id(0)
    bkt = bkt_ref[...]
    accs = [jnp.zeros(bkt.shape, F32) for _ in range(HEADS_PER_GROUP)]
    for b in range(N_BUCKETS):
        hit = bkt == b
        for h in range(HEADS_PER_GROUP):
            accs[h] = jnp.where(hit, tab_ref[b, g * HEADS_PER_GROUP + h], accs[h])
    for h in range(HEADS_PER_GROUP):
        o_ref[h] = accs[h]


def _bias_prompt(rel_bias):
    i = np.arange(BAND)[:, None]
    j = np.arange(2 * BAND)[None, :]
    dsub = np.maximum(i + BAND - j, 0)
    bkt = np.stack([_rel_bucket_np(dsub * dil) for _, dil in DIL_GROUPS])
    return pl.pallas_call(
        _bias_prompt_kernel,
        grid=(N_GROUPS,),
        in_specs=[pl.BlockSpec((None, BAND, 2 * BAND), lambda g: (g, 0, 0)),
                  pl.BlockSpec(memory_space=pltpu.SMEM)],
        out_specs=pl.BlockSpec((None, HEADS_PER_GROUP, BAND, 2 * BAND), lambda g: (g, 0, 0, 0)),
        out_shape=jax.ShapeDtypeStruct((N_GROUPS, HEADS_PER_GROUP, BAND, 2 * BAND), F32),
        compiler_params=_cparams(("arbitrary",)),
        name="bias_prompt",
    )(jnp.asarray(bkt), rel_bias)


SAMPLE_BIAS_ROWS = BAND + 8


def _bias_sample_kernel(bkt_ref, tab_ref, o_ref):
    bkt = bkt_ref[...]
    acc = jnp.zeros(o_ref.shape, F32)
    for b in range(N_BUCKETS):
        acc = jnp.where(bkt == b, tab_ref[b:b + 1, :], acc)
    o_ref[...] = acc


def _bias_sample(rel_bias_padded):
    rows = []
    for _, dil in DIL_GROUPS:
        dist = np.concatenate([(BAND - np.arange(BAND)) * dil, np.zeros(8, np.int64)])
        rows.append(_rel_bucket_np(dist))
    bkt = np.stack(rows)[:, :, None]
    return pl.pallas_call(
        _bias_sample_kernel,
        grid=(N_GROUPS,),
        in_specs=[pl.BlockSpec((None, SAMPLE_BIAS_ROWS, 1), lambda g: (g, 0, 0)),
                  pl.BlockSpec((N_BUCKETS, 128), lambda g: (0, 0))],
        out_specs=pl.BlockSpec((None, SAMPLE_BIAS_ROWS, 128), lambda g: (g, 0, 0)),
        out_shape=jax.ShapeDtypeStruct((N_GROUPS, SAMPLE_BIAS_ROWS, 128), F32),
        compiler_params=_cparams(("arbitrary",)),
        name="bias_sample",
    )(jnp.asarray(bkt), rel_bias_padded)


def _dil_attn_kernel(q_ref, kp_ref, kc_ref, vp_ref, vc_ref, bias_ref, o_ref, lse_ref):
    n = pl.program_id(2)
    i = lax.broadcasted_iota(jnp.int32, (BAND, 2 * BAND), 0)
    j = lax.broadcasted_iota(jnp.int32, (BAND, 2 * BAND), 1)
    valid = (j >= i) & (j <= i + BAND) & ((j >= BAND) | (n > 0))
    lse_ref[...] = jnp.zeros(lse_ref.shape, F32)
    for h in range(HEADS_PER_GROUP):
        hs = slice(h * HEAD_DIM, (h + 1) * HEAD_DIM)
        q = q_ref[:, hs]
        kk = jnp.concatenate([kp_ref[:, hs], kc_ref[:, hs]], axis=0)
        vv = jnp.concatenate([vp_ref[:, hs], vc_ref[:, hs]], axis=0)
        s = _dot_nt(q, kk) * SCALE + bias_ref[h]
        s = jnp.where(valid, s, NEG)
        m = jnp.max(s, axis=-1, keepdims=True)
        p = jnp.exp(s - m)
        l = jnp.sum(p, axis=-1, keepdims=True)
        o_ref[:, hs] = _dot(p, vv) / l
        lse_ref[:, h:h + 1] = m + jnp.log(l)


def _dil_attn(proj, bias_g, g, batch, seq):
    win, dil = DIL_GROUPS[g]
    assert win // dil == BAND and seq % (BAND * dil) == 0
    rows = seq // dil
    nblk = rows // BAND
    cpr = IN_A // WIDTH_A_OUT
    view = proj.reshape(batch, rows, dil * IN_A)
    blk = (None, BAND, WIDTH_A_OUT)
    qoff, koff, voff = g, WIDTH_A_QKV // WIDTH_A_OUT + g, 2 * WIDTH_A_QKV // WIDTH_A_OUT + g
    cur = lambda off: (lambda b, r, n: (b, n, r * cpr + off))
    prev = lambda off: (lambda b, r, n: (b, jnp.maximum(n - 1, 0), r * cpr + off))
    o, lse = pl.pallas_call(
        _dil_attn_kernel,
        grid=(batch, dil, nblk),
        in_specs=[pl.BlockSpec(blk, cur(qoff)),
                  pl.BlockSpec(blk, prev(koff)), pl.BlockSpec(blk, cur(koff)),
                  pl.BlockSpec(blk, prev(voff)), pl.BlockSpec(blk, cur(voff)),
                  pl.BlockSpec((None, HEADS_PER_GROUP, BAND, 2 * BAND), lambda b, r, n: (g, 0, 0, 0))],
        out_specs=[pl.BlockSpec((None, BAND, WIDTH_A_OUT), lambda b, r, n: (b, n, r)),
                   pl.BlockSpec((None, BAND, 128), lambda b, r, n: (b, n, r))],
        out_shape=[jax.ShapeDtypeStruct((batch, rows, dil * WIDTH_A_OUT), F32),
                   jax.ShapeDtypeStruct((batch, rows, dil * 128), F32)],
        compiler_params=_cparams(("parallel", "parallel", "arbitrary")),
        name=f"dil_attn_g{g}",
    )(view, view, view, view, view, bias_g)
    return o.reshape(batch * seq, WIDTH_A_OUT), lse.reshape(batch * seq, 128)


def _combine_kernel(o0_ref, o1_ref, o2_ref, l0_ref, l1_ref, l2_ref, o_ref):
    l0, l1, l2 = l0_ref[...], l1_ref[...], l2_ref[...]
    m = jnp.maximum(jnp.maximum(l0, l1), l2)
    e0, e1, e2 = jnp.exp(l0 - m), jnp.exp(l1 - m), jnp.exp(l2 - m)
    inv = 1.0 / (e0 + e1 + e2)
    a0, a1, a2 = e0 * inv, e1 * inv, e2 * inv
    for h in range(HEADS_PER_GROUP):
        hs = slice(h * HEAD_DIM, (h + 1) * HEAD_DIM)
        c = slice(h, h + 1)
        o_ref[:, hs] = a0[:, c] * o0_ref[:, hs] + a1[:, c] * o1_ref[:, hs] + a2[:, c] * o2_ref[:, hs]


def _combine(outs, lses, tm):
    m = outs[0].shape[0]
    ob = pl.BlockSpec((tm, WIDTH_A_OUT), lambda i: (i, 0))
    lb = pl.BlockSpec((tm, 128), lambda i: (i, 0))
    return pl.pallas_call(
        _combine_kernel,
        grid=(m // tm,),
        in_specs=[ob, ob, ob, lb, lb, lb],
        out_specs=ob,
        out_shape=jax.ShapeDtypeStruct((m, WIDTH_A_OUT), F32),
        compiler_params=_cparams(("parallel",)),
        name="combine_groups",
    )(*outs, *lses)


def _mem_attn_kernel(q_ref, k_ref, v_ref, o_ref):
    for h in range(MEM_HEADS):
        hs = slice(h * HEAD_DIM, (h + 1) * HEAD_DIM)
        s = _dot_nt(q_ref[:, hs], k_ref[:, hs]) * SCALE
        m = jnp.max(s, axis=-1, keepdims=True)
        p = jnp.exp(s - m)
        l = jnp.sum(p, axis=-1, keepdims=True)
        o_ref[:, hs] = _dot(p, v_ref[:, hs]) / l


def _mem_attn(proj, qcol, mk, mv, layer, batch, seq, tm):
    steps = seq // tm
    return pl.pallas_call(
        _mem_attn_kernel,
        grid=(batch, steps),
        in_specs=[pl.BlockSpec((tm, WIDTH_MEM), lambda b, i: (b * steps + i, qcol)),
                  pl.BlockSpec((None, N_MEM, WIDTH_MEM), lambda b, i: (layer, b, 0)),
                  pl.BlockSpec((None, N_MEM, WIDTH_MEM), lambda b, i: (layer, b, 0))],
        out_specs=pl.BlockSpec((tm, WIDTH_MEM), lambda b, i: (b * steps + i, 0)),
        out_shape=jax.ShapeDtypeStruct((batch * seq, WIDTH_MEM), F32),
        compiler_params=_cparams(("parallel", "parallel")),
        name="mem_attn",
    )(proj, mk, mv)


def _out_proj_kernel(a1_ref, a2_ref, z1_ref, z2_ref, x_ref, w1_ref, w2_ref, o_ref):
    g1 = (a1_ref[...] * _silu(z1_ref[...])).astype(BF16)
    g2 = (a2_ref[...] * _silu(z2_ref[...])).astype(BF16)
    y = jnp.dot(g1, w1_ref[...], preferred_element_type=F32)
    y = y + jnp.dot(g2, w2_ref[...], preferred_element_type=F32)
    o_ref[...] = x_ref[...] + y


def _out_proj(a1, a2, proj, z1col, z2col, x, w1, w2, tm):
    m, d = x.shape
    k1, k2 = a1.shape[1], a2.shape[1]
    return pl.pallas_call(
        _out_proj_kernel,
        grid=(m // tm,),
        in_specs=[pl.BlockSpec((tm, k1), lambda i: (i, 0)),
                  pl.BlockSpec((tm, k2), lambda i: (i, 0)),
                  pl.BlockSpec((tm, k1), lambda i: (i, z1col)),
                  pl.BlockSpec((tm, k2), lambda i: (i, z2col)),
                  pl.BlockSpec((tm, d), lambda i: (i, 0)),
                  pl.BlockSpec((k1, d), lambda i: (0, 0)),
                  pl.BlockSpec((k2, d), lambda i: (0, 0))],
        out_specs=pl.BlockSpec((tm, d), lambda i: (i, 0)),
        out_shape=jax.ShapeDtypeStruct((m, d), F32),
        input_output_aliases={4: 0},
        compiler_params=_cparams(("parallel",)),
        name="out_proj",
    )(a1, a2, proj, proj, x, w1, w2)


def _row_attend(q, ks, vs, bias_col, k0, v0, b0):
    s = jnp.sum(ks * q, axis=-1, keepdims=True) * SCALE + bias_col
    s0 = jnp.sum(k0 * q, axis=-1, keepdims=True) * SCALE + b0
    m = jnp.maximum(jnp.max(s, axis=0, keepdims=True), s0)
    p = jnp.exp(s - m)
    p0 = jnp.exp(s0 - m)
    l = jnp.sum(p, axis=0, keepdims=True) + p0
    acc = jnp.sum(p * vs, axis=0, keepdims=True) + p0 * v0
    return acc, m, l


def _shift_copy(src_ref, dst_ref, drop, chunk=256):
    total = src_ref.shape[0] - drop
    steps = -(-total // chunk)

    def body(i, carry):
        start = pl.multiple_of(jnp.minimum(i * chunk, total - chunk), 8)
        dst_ref[pl.ds(start, chunk), :] = src_ref[pl.ds(start + drop, chunk), :]
        return carry

    lax.fori_loop(0, steps, body, 0)


def _sample_win_kernel(new_ref, k0_ref, v0_ref, k1_ref, v1_ref, k2_ref, v2_ref, bias_ref,
                       ok0_ref, ov0_ref, ok1_ref, ov1_ref, ok2_ref, ov2_ref, o_ref,
                       kcat_ref, vcat_ref, *, n_new):
    H = HEADS_PER_GROUP
    cin = ((k0_ref, v0_ref), (k1_ref, v1_ref), (k2_ref, v2_ref))
    cout = ((ok0_ref, ov0_ref), (ok1_ref, ov1_ref), (ok2_ref, ov2_ref))
    qb, kb, vb = 0, N_GROUPS * H, 2 * N_GROUPS * H

    for g in range(N_GROUPS):
        for (src, dst, base) in ((cin[g][0], cout[g][0], kb), (cin[g][1], cout[g][1], vb)):
            rows = src.shape[0]
            _shift_copy(src, dst, n_new * H)
            for t in range(n_new):
                dst[rows - (n_new - t) * H: rows - (n_new - t - 1) * H, :] = \
                    new_ref[t, base + g * H: base + (g + 1) * H, :]

    rows0 = k0_ref.shape[0]
    kcat_ref[0:rows0, :] = k0_ref[...]
    vcat_ref[0:rows0, :] = v0_ref[...]
    for t in range(n_new):
        kcat_ref[rows0 + t * H: rows0 + (t + 1) * H, :] = new_ref[t, kb:kb + H, :]
        vcat_ref[rows0 + t * H: rows0 + (t + 1) * H, :] = new_ref[t, vb:vb + H, :]

    for t in range(n_new):
        for h in range(H):
            parts = []
            for g, (win, dil) in enumerate(DIL_GROUPS):
                c = g * H + h
                q = new_ref[t, qb + c: qb + c + 1, :]
                k_self = new_ref[t, kb + c: kb + c + 1, :]
                v_self = new_ref[t, vb + c: vb + c + 1, :]
                ksrc, vsrc = (kcat_ref, vcat_ref) if g == 0 else cin[g]
                sl = pl.ds(t * H + h, BAND, stride=dil * H)
                parts.append(_row_attend(q, ksrc[sl, :], vsrc[sl, :],
                                         bias_ref[g, 0:BAND, c:c + 1],
                                         k_self, v_self, bias_ref[g, BAND:BAND + 1, c:c + 1]))
            m = jnp.maximum(jnp.maximum(parts[0][1], parts[1][1]), parts[2][1])
            num = jnp.zeros((1, HEAD_DIM), F32)
            den = jnp.zeros((1, 1), F32)
            for acc, mg, lg in parts:
                w = jnp.exp(mg - m)
                num = num + w * acc
                den = den + w * lg
            o_ref[t, h:h + 1, :] = num / den


def _sample_win(proj_s, caches, bias_s, batch, n_new):
    H = HEADS_PER_GROUP
    for g, (win, dil) in enumerate(DIL_GROUPS):
        assert caches[2 * g].shape[1] == win * H and win // dil == BAND and n_new * H % 8 == 0
    new = proj_s.reshape(batch, n_new, IN_A // 128, 128)
    cspecs = [pl.BlockSpec((None,) + c.shape[1:], lambda b: (b, 0, 0)) for c in caches]
    rows0 = caches[0].shape[1]
    outs = pl.pallas_call(
        functools.partial(_sample_win_kernel, n_new=n_new),
        grid=(batch,),
        in_specs=[pl.BlockSpec((None, n_new, IN_A // 128, 128), lambda b: (b, 0, 0, 0))] + cspecs
                 + [pl.BlockSpec(bias_s.shape, lambda b: (0, 0, 0))],
        out_specs=cspecs + [pl.BlockSpec((None, n_new, H, HEAD_DIM), lambda b: (b, 0, 0, 0))],
        out_shape=[jax.ShapeDtypeStruct(c.shape, F32) for c in caches]
                  + [jax.ShapeDtypeStruct((batch, n_new, H, HEAD_DIM), F32)],
        scratch_shapes=[pltpu.VMEM((rows0 + n_new * H, 128), F32),
                        pltpu.VMEM((rows0 + n_new * H, 128), F32)],
        compiler_params=_cparams(("parallel",)),
        name="sample_win_attn",
    )(new, *caches, bias_s)
    return outs[:6], outs[6].reshape(batch * n_new, WIDTH_A_OUT)


def _sample_mem_kernel(q_ref, k_ref, v_ref, o_ref, *, n_new, qblk):
    for t in range(n_new):
        for h in range(MEM_HEADS):
            q = q_ref[t, qblk + h: qblk + h + 1, :]
            sl = pl.ds(h, N_MEM, stride=MEM_HEADS)
            ks, vs = k_ref[sl, :], v_ref[sl, :]
            s = jnp.sum(ks * q, axis=-1, keepdims=True) * SCALE
            m = jnp.max(s, axis=0, keepdims=True)
            p = jnp.exp(s - m)
            l = jnp.sum(p, axis=0, keepdims=True)
            o_ref[t, h:h + 1, :] = jnp.sum(p * vs, axis=0, keepdims=True) / l


def _sample_mem(proj_s, qcol512, mem_k, mem_v, layer, batch, n_new):
    ncol = proj_s.shape[1] // 128
    new = proj_s.reshape(batch, n_new, ncol, 128)
    mspec = pl.BlockSpec((None, None, N_MEM * MEM_HEADS, HEAD_DIM), lambda b: (layer, b, 0, 0))
    out = pl.pallas_call(
        functools.partial(_sample_mem_kernel, n_new=n_new, qblk=qcol512 * 4),
        grid=(batch,),
        in_specs=[pl.BlockSpec((None, n_new, ncol, 128), lambda b: (b, 0, 0, 0)), mspec, mspec],
        out_specs=pl.BlockSpec((None, n_new, MEM_HEADS, HEAD_DIM), lambda b: (b, 0, 0, 0)),
        out_shape=jax.ShapeDtypeStruct((batch, n_new, MEM_HEADS, HEAD_DIM), F32),
        compiler_params=_cparams(("parallel",)),
        name="sample_mem_attn",
    )(new, mem_k, mem_v)
    return out.reshape(batch * n_new, WIDTH_MEM)


def _conv_qkv(xs_ref, cw_ref, base, rows):
    acc = xs_ref[pl.ds(base - (CONV_W - 1), rows), :] * cw_ref[0:1, :]
    for wi in range(1, CONV_W):
        acc = acc + xs_ref[pl.ds(base - (CONV_W - 1) + wi, rows), :] * cw_ref[wi:wi + 1, :]
    return _silu(acc)


def _l2norm(x):
    return x * lax.rsqrt(jnp.sum(x * x, axis=-1, keepdims=True) + EPS)


def _gate_terms(ba, alog_row, dtb_row):
    beta = _sigmoid(ba)
    g = -jnp.exp(alog_row) * _softplus(ba + dtb_row)
    return beta, g


def _delta_prompt_kernel(x_ref, halo_ref, ba_ref, cw_ref, alog_ref, dtb_ref, onw_ref,
                         o_ref, s_out_ref, xs_ref, s_ref):
    C = BLOCK_ROWS
    it = pl.program_id(1)

    @pl.when(it == 0)
    def _():
        s_ref[...] = jnp.zeros(s_ref.shape, F32)
        xs_ref[0:8, :] = jnp.zeros((8, xs_ref.shape[1]), F32)

    @pl.when(it > 0)
    def _():
        xs_ref[0:8, :] = halo_ref[...]

    xs_ref[8:8 + C, :] = x_ref[...]
    conv = _conv_qkv(xs_ref, cw_ref, 8, C)

    beta, g = _gate_terms(ba_ref[...], alog_ref[...], dtb_ref[...])
    row = lax.broadcasted_iota(jnp.int32, (C, C), 0)
    col = lax.broadcasted_iota(jnp.int32, (C, C), 1)
    tri = row >= col
    strict = row > col
    ones_tri = jnp.where(tri, 1.0, 0.0).astype(BF16)
    gh, gm, gl = _split3(g)
    d = lambda a, b: jnp.dot(a, b, preferred_element_type=F32)
    gc = d(ones_tri, gh) + (d(ones_tri, gm) + d(ones_tri, gl))
    gc_t = gc.T
    eye = jnp.where(row == col, 1.0, 0.0)

    for h in range(N_HEADS_B):
        hs = slice(h * HEAD_DIM, (h + 1) * HEAD_DIM)
        q = _l2norm(conv[:, h * HEAD_DIM:(h + 1) * HEAD_DIM]) * SCALE
        k = _l2norm(conv[:, WIDTH_B + h * HEAD_DIM: WIDTH_B + (h + 1) * HEAD_DIM])
        v = conv[:, 2 * WIDTH_B + h * HEAD_DIM: 2 * WIDTH_B + (h + 1) * HEAD_DIM]
        b_col = beta[:, h:h + 1]
        gc_col = gc[:, 8 + h: 9 + h]
        gc_row = gc_t[8 + h: 9 + h, :]
        g_last = gc_col[C - 1:C, :]
        decay = jnp.where(tri, jnp.exp(jnp.where(tri, gc_col - gc_row, 0.0)), 0.0)
        kb = k * b_col
        egc = jnp.exp(gc_col)
        lmat = jnp.where(strict, _dot_nt(kb, k) * decay, 0.0)
        mpow = -lmat
        tinv = eye + mpow
        for _ in range(int(math.log2(C)) - 1):
            mpow = _dot_precise(mpow, mpow)
            tinv = tinv + _dot_precise(tinv, mpow)
        u = _dot(tinv, v * b_col)
        w = _dot(tinv, kb * egc)
        aqk = _dot_nt(q, k) * decay
        qg = q * egc
        kdec = k * jnp.exp(g_last - gc_col)
        s_prev = s_ref[h]
        v_new = u - _dot(w, s_prev)
        o = _dot(qg, s_prev) + _dot(aqk, v_new)
        s_ref[h] = s_prev * jnp.exp(g_last) + _dot(kdec.T, v_new)
        ms = jnp.mean(o * o, axis=-1, keepdims=True)
        o_ref[:, hs] = o * lax.rsqrt(ms + EPS) * onw_ref[...]

    @pl.when(it == pl.num_programs(1) - 1)
    def _():
        s_out_ref[...] = s_ref[...]


def _delta_prompt(proj, conv_w, alog_row, dtb_row, onw, batch, seq):
    C = BLOCK_ROWS
    steps = seq // C
    wq = 3 * WIDTH_B
    row1 = lambda shape: pl.BlockSpec(shape, lambda b, i: (0, 0))
    return pl.pallas_call(
        _delta_prompt_kernel,
        grid=(batch, steps),
        in_specs=[pl.BlockSpec((C, wq), lambda b, i: (b * steps + i, 0)),
                  pl.BlockSpec((8, wq), lambda b, i: (jnp.maximum((b * steps + i) * (C // 8) - 1, 0), 0)),
                  pl.BlockSpec((C, 128), lambda b, i: (b * steps + i, BA_COL_BLOCK)),
                  row1((CONV_W, wq)), row1((1, 128)), row1((1, 128)), row1((1, HEAD_DIM))],
        out_specs=[pl.BlockSpec((C, WIDTH_B), lambda b, i: (b * steps + i, 0)),
                   pl.BlockSpec((None, N_HEADS_B, HEAD_DIM, HEAD_DIM), lambda b, i: (b, 0, 0, 0))],
        out_shape=[jax.ShapeDtypeStruct((batch * seq, WIDTH_B), F32),
                   jax.ShapeDtypeStruct((batch, N_HEADS_B, HEAD_DIM, HEAD_DIM), F32)],
        scratch_shapes=[pltpu.VMEM((8 + C, wq), F32),
                        pltpu.VMEM((N_HEADS_B, HEAD_DIM, HEAD_DIM), F32)],
        compiler_params=_cparams(("parallel", "arbitrary")),
        name="delta_prompt",
    )(proj, proj, proj, conv_w, alog_row, dtb_row, onw)


def _delta_sample_kernel(x_ref, cs_ref, ba_ref, s_in_ref, cw_ref, alog_ref, dtb_ref, onw_ref,
                         o_ref, s_out_ref, xs_ref, a_ref, *, n_new):
    xs_ref[0:8, :] = jnp.zeros((8, xs_ref.shape[1]), F32)
    xs_ref[8 - (CONV_W - 1):8, :] = cs_ref[...]
    xs_ref[8:16, :] = jnp.zeros((8, xs_ref.shape[1]), F32)
    xs_ref[8:8 + n_new, :] = x_ref[...]
    conv = _conv_qkv(xs_ref, cw_ref, 8, 8)
    beta, g = _gate_terms(ba_ref[...], alog_ref[...], dtb_ref[...])
    eg = jnp.exp(g)

    a_ref[...] = jnp.zeros(a_ref.shape, F32)
    for t in range(n_new):
        for h in range(N_HEADS_B):
            q = _l2norm(conv[t:t + 1, h * HEAD_DIM:(h + 1) * HEAD_DIM]) * SCALE
            k = _l2norm(conv[t:t + 1, WIDTH_B + h * HEAD_DIM: WIDTH_B + (h + 1) * HEAD_DIM])
            r = t * N_HEADS_B + h
            a_ref[0, r:r + 1, :] = q
            a_ref[1, r:r + 1, :] = k
    q_t = a_ref[0].T
    k_t = a_ref[1].T

    for h in range(N_HEADS_B):
        s = s_in_ref[h]
        for t in range(n_new):
            r = t * N_HEADS_B + h
            k_col = k_t[:, r:r + 1]
            q_col = q_t[:, r:r + 1]
            v = conv[t:t + 1, 2 * WIDTH_B + h * HEAD_DIM: 2 * WIDTH_B + (h + 1) * HEAD_DIM]
            sd = s * eg[t:t + 1, 8 + h: 9 + h]
            ks = jnp.sum(sd * k_col, axis=0, keepdims=True)
            dv = (v - ks) * beta[t:t + 1, h:h + 1]
            s = sd + k_col * dv
            o = jnp.sum(s * q_col, axis=0, keepdims=True)
            ms = jnp.mean(o * o, axis=-1, keepdims=True)
            o_ref[t, h:h + 1, :] = o * lax.rsqrt(ms + EPS) * onw_ref[...]
        s_out_ref[h] = s


def _delta_sample(proj_s, conv_state, delta_state, conv_w, alog_row, dtb_row, onw, batch, n_new):
    assert n_new * N_HEADS_B <= HEAD_DIM and n_new <= 8
    wq = 3 * WIDTH_B
    new = proj_s.reshape(batch, n_new, IN_B_PAD)
    c1 = lambda shape: pl.BlockSpec(shape, lambda b: (0, 0))
    sspec = pl.BlockSpec((None, N_HEADS_B, HEAD_DIM, HEAD_DIM), lambda b: (b, 0, 0, 0))
    o, s = pl.pallas_call(
        functools.partial(_delta_sample_kernel, n_new=n_new),
        grid=(batch,),
        in_specs=[pl.BlockSpec((None, n_new, wq), lambda b: (b, 0, 0)),
                  pl.BlockSpec((None, CONV_W - 1, wq), lambda b: (b, 0, 0)),
                  pl.BlockSpec((None, n_new, 128), lambda b: (b, 0, BA_COL_BLOCK)),
                  sspec, c1((CONV_W, wq)), c1((1, 128)), c1((1, 128)), c1((1, HEAD_DIM))],
        out_specs=[pl.BlockSpec((None, n_new, N_HEADS_B, HEAD_DIM), lambda b: (b, 0, 0, 0)), sspec],
        out_shape=[jax.ShapeDtypeStruct((batch, n_new, N_HEADS_B, HEAD_DIM), F32),
                   jax.ShapeDtypeStruct(delta_state.shape, F32)],
        scratch_shapes=[pltpu.VMEM((16, wq), F32), pltpu.VMEM((2, HEAD_DIM, HEAD_DIM), F32)],
        compiler_params=_cparams(("parallel",)),
        name="delta_sample",
    )(new, conv_state, new, delta_state, conv_w, alog_row, dtb_row, onw)
    return o.reshape(batch * n_new, WIDTH_B), s


def _reorder_w_in_b(w):
    q0 = 3 * WIDTH_B
    b0, a0, m0, z0 = q0, q0 + N_HEADS_B, q0 + 2 * N_HEADS_B, q0 + 2 * N_HEADS_B + WIDTH_MEM
    parts = [w[:, :q0], w[:, z0:], w[:, m0:z0], w[:, b0:a0], w[:, a0:m0]]
    out = jnp.concatenate(parts, axis=1)
    return jnp.pad(out, ((0, 0), (0, IN_B_PAD - out.shape[1])))


def _lane_row(vals, offset):
    return jnp.zeros((1, 128), F32).at[0, offset:offset + vals.shape[0]].set(vals)


def kernel(x_prompt, x_sample, cache_win_k0, cache_win_v0, cache_win_k1, cache_win_v1, cache_win_k2, cache_win_v2, state_conv, state_delta, cache_mem_k, cache_mem_v, mem_prompt, norm_w, final_norm_w, rel_bias, w_in_a, w_out_a, w_in_b, conv_w, a_log, dt_bias, o_norm_w, w_out_b, w_mem_kv):
    bp, seq, d = x_prompt.shape
    bs, n_new, _ = x_sample.shape
    depth = norm_w.shape[0]
    H = HEADS_PER_GROUP
    tm_p = 1024

    xp = x_prompt.reshape(bp * seq, d)
    xs = x_sample.reshape(bs * n_new, d)

    mk_all, mv_all = _mem_kv(mem_prompt.reshape(bp * N_MEM, d), w_mem_kv.astype(BF16))
    bias_p = _bias_prompt(rel_bias)
    bias_s = _bias_sample(jnp.pad(rel_bias, ((0, 0), (0, 128 - rel_bias.shape[1]))))
    mem_k_s = cache_mem_k.reshape(depth, bs, N_MEM * MEM_HEADS, HEAD_DIM)
    mem_v_s = cache_mem_v.reshape(depth, bs, N_MEM * MEM_HEADS, HEAD_DIM)
    caches_in = (cache_win_k0, cache_win_v0, cache_win_k1, cache_win_v1, cache_win_k2, cache_win_v2)

    p_win = [[] for _ in range(2 * N_GROUPS)]
    s_win = [[] for _ in range(2 * N_GROUPS)]
    p_conv, p_delta, s_conv, s_delta = [], [], [], []

    for i in range(depth):
        li = i // 2
        nw = norm_w[i].reshape(1, d)
        if i % 2 == 0:
            w_in = w_in_a[li].astype(BF16)
            w_out = w_out_a[li].astype(BF16)
            proj_p = _norm_proj(xp, nw, w_in, tm_p, 2048)
            proj_s = _norm_proj(xs, nw, w_in, bs * n_new, 2048)
            outs, lses = [], []
            for g, (win, dil) in enumerate(DIL_GROUPS):
                o_g, lse_g = _dil_attn(proj_p, bias_p, g, bp, seq)
                outs.append(o_g)
                lses.append(lse_g)
                keep = min(win, seq)
                p3 = proj_p.reshape(bp, seq, IN_A)
                kcol = WIDTH_A_QKV + g * WIDTH_A_OUT
                vcol = 2 * WIDTH_A_QKV + g * WIDTH_A_OUT
                p_win[2 * g].append(p3[:, seq - keep:, kcol:kcol + WIDTH_A_OUT].reshape(bp, keep, H, HEAD_DIM))
                p_win[2 * g + 1].append(p3[:, seq - keep:, vcol:vcol + WIDTH_A_OUT].reshape(bp, keep, H, HEAD_DIM))
            o_p = _combine(outs, lses, tm_p)
            om_p = _mem_attn(proj_p, 9, mk_all, mv_all, i, bp, seq, 512)
            xp = _out_proj(o_p, om_p, proj_p, 10, 11, xp, w_out[:WIDTH_A_OUT], w_out[WIDTH_A_OUT:], 512)
            cin = [c[li].reshape(bs, c.shape[2] * H, HEAD_DIM) for c in caches_in]
            new_caches, o_s = _sample_win(proj_s, cin, bias_s, bs, n_new)
            for n in range(2 * N_GROUPS):
                s_win[n].append(new_caches[n].reshape(caches_in[n].shape[1:]))
            om_s = _sample_mem(proj_s, 9, mem_k_s, mem_v_s, i, bs, n_new)
            xs = _out_proj(o_s, om_s, proj_s, 10, 11, xs, w_out[:WIDTH_A_OUT], w_out[WIDTH_A_OUT:], bs * n_new)
        else:
            w_in = _reorder_w_in_b(w_in_b[li]).astype(BF16)
            w_out = w_out_b[li].astype(BF16)
            alog_row = _lane_row(a_log[li], N_HEADS_B)
            dtb_row = _lane_row(dt_bias[li], N_HEADS_B)
            onw = o_norm_w[li].reshape(1, HEAD_DIM)
            proj_p = _norm_proj(xp, nw, w_in, tm_p, 1792)
            proj_s = _norm_proj(xs, nw, w_in, bs * n_new, 1792)
            wq = 3 * WIDTH_B
            o_p, s_p = _delta_prompt(proj_p, conv_w[li], alog_row, dtb_row, onw, bp, seq)
            p_delta.append(s_p)
            p_conv.append(proj_p.reshape(bp, seq, IN_B_PAD)[:, seq - (CONV_W - 1):, :wq])
            om_p = _mem_attn(proj_p, 9, mk_all, mv_all, i, bp, seq, 512)
            xp = _out_proj(o_p, om_p, proj_p, 3, 8, xp, w_out[:WIDTH_B], w_out[WIDTH_B:], 512)
            o_s, s_s = _delta_sample(proj_s, state_conv[li], state_delta[li], conv_w[li],
                                     alog_row, dtb_row, onw, bs, n_new)
            s_delta.append(s_s)
            xcat = jnp.concatenate([state_conv[li], proj_s.reshape(bs, n_new, IN_B_PAD)[:, :, :wq]], axis=1)
            s_conv.append(xcat[:, n_new:])
            om_s = _sample_mem(proj_s, 9, mem_k_s, mem_v_s, i, bs, n_new)
            xs = _out_proj(o_s, om_s, proj_s, 3, 8, xs, w_out[:WIDTH_B], w_out[WIDTH_B:], bs * n_new)

    fnw = final_norm_w.reshape(1, d)
    y_prompt = _final_norm(xp, fnw, tm_p).reshape(bp, seq, d)
    y_sample = _final_norm(xs, fnw, bs * n_new).reshape(bs, n_new, d)
    p_mk = mk_all.reshape(depth, bp, N_MEM, MEM_HEADS, HEAD_DIM)
    p_mv = mv_all.reshape(depth, bp, N_MEM, MEM_HEADS, HEAD_DIM)
    return (y_prompt, y_sample,
            jnp.stack(p_win[0]), jnp.stack(p_win[1]), jnp.stack(p_win[2]),
            jnp.stack(p_win[3]), jnp.stack(p_win[4]), jnp.stack(p_win[5]),
            jnp.stack(p_conv), jnp.stack(p_delta), p_mk, p_mv,
            jnp.stack(s_win[0]), jnp.stack(s_win[1]), jnp.stack(s_win[2]),
            jnp.stack(s_win[3]), jnp.stack(s_win[4]), jnp.stack(s_win[5]),
            jnp.stack(s_conv), jnp.stack(s_delta))
```

```python
import functools
import math

import jax
import jax.numpy as jnp
import numpy as np
from jax import lax
from jax.experimental import pallas as pl
from jax.experimental.pallas import tpu as pltpu

F32 = jnp.float32
BF16 = jnp.bfloat16

D_MODEL = 1024
HEAD_DIM = 128
DIL_GROUPS = ((128, 1), (512, 4), (2048, 16))
N_GROUPS = len(DIL_GROUPS)
HEADS_PER_GROUP = 4
WIDTH_A_QKV = N_GROUPS * HEADS_PER_GROUP * HEAD_DIM
WIDTH_A_OUT = HEADS_PER_GROUP * HEAD_DIM
N_MEM = 256
MEM_HEADS = 4
WIDTH_MEM = MEM_HEADS * HEAD_DIM
N_HEADS_B = 8
WIDTH_B = N_HEADS_B * HEAD_DIM
CONV_W = 4
N_BUCKETS = 32
MAX_EXACT = N_BUCKETS // 2
MAX_DIST = 2048
EPS = 1e-6
NEG = -1e30
SCALE = HEAD_DIM ** -0.5
BAND = 128
BLOCK_ROWS = 128
ATTN_TILE = BAND * max(d for _, d in DIL_GROUPS)
IN_A = 3 * WIDTH_A_QKV + WIDTH_MEM + WIDTH_A_OUT + WIDTH_MEM
IN_B_PAD = 5376
BA_COL_BLOCK = (3 * WIDTH_B + WIDTH_B + WIDTH_MEM + WIDTH_MEM) // 128
VMEM_LIMIT = 48 * 1024 * 1024


def _cparams(sem):
    return pltpu.CompilerParams(dimension_semantics=sem, vmem_limit_bytes=VMEM_LIMIT)


def _sigmoid(x):
    return 1.0 / (1.0 + jnp.exp(-x))


def _silu(x):
    return x * _sigmoid(x)


def _softplus(x):
    return jnp.maximum(x, 0.0) + jnp.log(1.0 + jnp.exp(-jnp.abs(x)))


def _mm(a, b):
    return jnp.dot(a, b, preferred_element_type=F32)


def _dot(a, b):
    return _mm(a.astype(BF16), b.astype(BF16))


def _dot_nt(a, b):
    return lax.dot_general(a.astype(BF16), b.astype(BF16), (((1,), (1,)), ((), ())),
                           preferred_element_type=F32)


def _split2(a):
    hi = a.astype(BF16)
    lo = (a - hi.astype(F32)).astype(BF16)
    return hi, lo


def _split3(a):
    hi = a.astype(BF16)
    r1 = a - hi.astype(F32)
    mid = r1.astype(BF16)
    lo = (r1 - mid.astype(F32)).astype(BF16)
    return hi, mid, lo


def _l2norm(x):
    return x * lax.rsqrt(jnp.sum(x * x, axis=-1, keepdims=True) + EPS)


def _norm_proj_kernel(x_ref, nw_ref, w_ref, o_ref, h_ref):
    @pl.when(pl.program_id(1) == 0)
    def _():
        x = x_ref[...]
        ms = jnp.mean(x * x, axis=-1, keepdims=True)
        h_ref[...] = (x * lax.rsqrt(ms + EPS) * nw_ref[...]).astype(BF16)

    o_ref[...] = _mm(h_ref[...], w_ref[...])


def _norm_proj(x, nw, w, tm, tn):
    m, d = x.shape
    n = w.shape[1]
    return pl.pallas_call(
        _norm_proj_kernel,
        grid=(m // tm, n // tn),
        in_specs=[pl.BlockSpec((tm, d), lambda i, j: (i, 0)),
                  pl.BlockSpec((1, d), lambda i, j: (0, 0)),
                  pl.BlockSpec((d, tn), lambda i, j: (0, j))],
        out_specs=pl.BlockSpec((tm, tn), lambda i, j: (i, j)),
        out_shape=jax.ShapeDtypeStruct((m, n), F32),
        scratch_shapes=[pltpu.VMEM((tm, d), BF16)],
        compiler_params=_cparams(("parallel", "arbitrary")),
        name="norm_proj",
    )(x, nw, w)


def _final_norm_kernel(x_ref, nw_ref, o_ref):
    x = x_ref[...]
    ms = jnp.mean(x * x, axis=-1, keepdims=True)
    o_ref[...] = x * lax.rsqrt(ms + EPS) * nw_ref[...]


def _final_norm(x, nw, tm):
    m, d = x.shape
    return pl.pallas_call(
        _final_norm_kernel,
        grid=(m // tm,),
        in_specs=[pl.BlockSpec((tm, d), lambda i: (i, 0)),
                  pl.BlockSpec((1, d), lambda i: (0, 0))],
        out_specs=pl.BlockSpec((tm, d), lambda i: (i, 0)),
        out_shape=jax.ShapeDtypeStruct((m, d), F32),
        compiler_params=_cparams(("parallel",)),
        name="final_norm",
    )(x, nw)


def _mem_kv_kernel(x_ref, w_ref, k_ref, v_ref):
    r = _mm(x_ref[...].astype(BF16), w_ref[...])
    k_ref[...] = r[:, :WIDTH_MEM]
    v_ref[...] = r[:, WIDTH_MEM:]


def _mem_kv(mem, w):
    m, d = mem.shape
    depth = w.shape[0]
    out = jax.ShapeDtypeStruct((depth, m, WIDTH_MEM), F32)
    return pl.pallas_call(
        _mem_kv_kernel,
        grid=(depth,),
        in_specs=[pl.BlockSpec((m, d), lambda i: (0, 0)),
                  pl.BlockSpec((None, d, 2 * WIDTH_MEM), lambda i: (i, 0, 0))],
        out_specs=[pl.BlockSpec((None, m, WIDTH_MEM), lambda i: (i, 0, 0)),
                   pl.BlockSpec((None, m, WIDTH_MEM), lambda i: (i, 0, 0))],
        out_shape=[out, out],
        compiler_params=_cparams(("parallel",)),
        name="mem_kv",
    )(mem, w)


def _rel_bucket_np(dist):
    dist = np.asarray(dist, np.int64)
    df = np.maximum(dist, 1).astype(np.float32)
    large = MAX_EXACT + (np.log(df / np.float32(MAX_EXACT)) / np.float32(math.log(MAX_DIST / MAX_EXACT))
                         * np.float32(N_BUCKETS - MAX_EXACT)).astype(np.int32)
    return np.where(dist < MAX_EXACT, dist, np.minimum(large, N_BUCKETS - 1)).astype(np.int32)


def _bias_prompt_kernel(bkt_ref, tab_ref, o_ref):
    g = pl.program_id(0)
    bkt = bkt_ref[...]
    accs = [jnp.zeros(bkt.shape, F32) for _ in range(HEADS_PER_GROUP)]
    for b in range(N_BUCKETS):
        hit = bkt == b
        for h in range(HEADS_PER_GROUP):
            accs[h] = jnp.where(hit, tab_ref[b, g * HEADS_PER_GROUP + h], accs[h])
    for h in range(HEADS_PER_GROUP):
        o_ref[h] = accs[h]


def _bias_prompt(rel_bias):
    i = np.arange(BAND)[:, None]
    j = np.arange(2 * BAND)[None, :]
    dsub = np.maximum(i + BAND - j, 0)
    bkt = np.stack([_rel_bucket_np(dsub * dil) for _, dil in DIL_GROUPS])
    return pl.pallas_call(
        _bias_prompt_kernel,
        grid=(N_GROUPS,),
        in_specs=[pl.BlockSpec((None, BAND, 2 * BAND), lambda g: (g, 0, 0)),
                  pl.BlockSpec(memory_space=pltpu.SMEM)],
        out_specs=pl.BlockSpec((None, HEADS_PER_GROUP, BAND, 2 * BAND), lambda g: (g, 0, 0, 0)),
        out_shape=jax.ShapeDtypeStruct((N_GROUPS, HEADS_PER_GROUP, BAND, 2 * BAND), F32),
        compiler_params=_cparams(("arbitrary",)),
        name="bias_prompt",
    )(jnp.asarray(bkt), rel_bias)


SAMPLE_BIAS_ROWS = BAND + 8


def _bias_sample_kernel(bkt_ref, tab_ref, o_ref):
    bkt = bkt_ref[...]
    acc = jnp.zeros(o_ref.shape, F32)
    for b in range(N_BUCKETS):
        acc = jnp.where(bkt == b, tab_ref[b:b + 1, :], acc)
    o_ref[...] = acc


def _bias_sample(rel_bias_padded):
    rows = []
    for _, dil in DIL_GROUPS:
        dist = np.concatenate([(BAND - np.arange(BAND)) * dil, np.zeros(8, np.int64)])
        rows.append(_rel_bucket_np(dist))
    bkt = np.stack(rows)[:, :, None]
    return pl.pallas_call(
        _bias_sample_kernel,
        grid=(N_GROUPS,),
        in_specs=[pl.BlockSpec((None, SAMPLE_BIAS_ROWS, 1), lambda g: (g, 0, 0)),
                  pl.BlockSpec((N_BUCKETS, 128), lambda g: (0, 0))],
        out_specs=pl.BlockSpec((None, SAMPLE_BIAS_ROWS, 128), lambda g: (g, 0, 0)),
        out_shape=jax.ShapeDtypeStruct((N_GROUPS, SAMPLE_BIAS_ROWS, 128), F32),
        compiler_params=_cparams(("arbitrary",)),
        name="bias_sample",
    )(jnp.asarray(bkt), rel_bias_padded)


def _prompt_attn_kernel(*refs):
    ins, (o_ref, acc_ref, ml_ref) = refs[:-3], refs[-3:]
    first_tile = pl.program_id(1) == 0
    row = lax.broadcasted_iota(jnp.int32, (BAND, BAND), 0)
    col = lax.broadcasted_iota(jnp.int32, (BAND, BAND), 1)
    keep_cur = col <= row
    keep_prev = col >= row
    keep_prev_first = keep_prev & jnp.logical_not(first_tile)

    for g, (_, dil) in enumerate(DIL_GROUPS):
        q_ref, kc_ref, vc_ref, kp_ref, vp_ref, bias_ref = ins[6 * g: 6 * g + 6]
        bias_prev = bias_ref[:, 0:BAND]
        bias_cur = bias_ref[:, BAND:2 * BAND]
        for n in range(ATTN_TILE // (BAND * dil)):
            for r in range(dil):
                cur = pl.ds(n * BAND * dil + r, BAND, stride=dil)
                q = q_ref[cur, :].astype(BF16)
                kc = kc_ref[cur, :]
                vc = vc_ref[cur, :]
                if n == 0:
                    prev = pl.ds(r, BAND, stride=dil)
                    kp, vp, mask_prev = kp_ref[prev, :], vp_ref[prev, :], keep_prev_first
                else:
                    prev = pl.ds((n - 1) * BAND * dil + r, BAND, stride=dil)
                    kp, vp, mask_prev = kc_ref[prev, :], vc_ref[prev, :], keep_prev
                s_c = jnp.where(keep_cur, _dot_nt(q, kc) * SCALE + bias_cur, NEG)
                s_p = jnp.where(mask_prev, _dot_nt(q, kp) * SCALE + bias_prev, NEG)
                m = jnp.maximum(jnp.max(s_c, axis=-1, keepdims=True), jnp.max(s_p, axis=-1, keepdims=True))
                p_c = jnp.exp(s_c - m)
                p_p = jnp.exp(s_p - m)
                l = jnp.sum(p_c, axis=-1, keepdims=True) + jnp.sum(p_p, axis=-1, keepdims=True)
                acc_ref[g, cur, :] = _dot(p_c, vc) + _dot(p_p, vp)
                ml_ref[g, cur, :] = m
                ml_ref[N_GROUPS + g, cur, :] = l

    chunk = 256
    for c in range(ATTN_TILE // chunk):
        rows = slice(c * chunk, (c + 1) * chunk)
        ms = [ml_ref[g, rows, :] for g in range(N_GROUPS)]
        m = functools.reduce(jnp.maximum, ms)
        num = jnp.zeros((chunk, HEAD_DIM), F32)
        den = jnp.zeros((chunk, 1), F32)
        for g in range(N_GROUPS):
            w = jnp.exp(ms[g] - m)
            num = num + w * acc_ref[g, rows, :]
            den = den + w * ml_ref[N_GROUPS + g, rows, :]
        o_ref[rows, :] = num / den


def _prompt_attn(proj, bias_p, batch, seq):
    H = HEADS_PER_GROUP
    assert seq % ATTN_TILE == 0
    ntile = seq // ATTN_TILE
    args, specs = [], []
    for g, (win, dil) in enumerate(DIL_GROUPS):
        assert win // dil == BAND
        span = BAND * dil
        per = ATTN_TILE // span
        qc, kc, vc = g * H, (N_GROUPS + g) * H, (2 * N_GROUPS + g) * H
        cur = lambda c: (lambda b, i, h: (b * ntile + i, c + h))
        prev = lambda c, per=per: (lambda b, i, h: (jnp.maximum((b * ntile + i) * per - 1, 0), c + h))
        specs += [pl.BlockSpec((ATTN_TILE, HEAD_DIM), cur(qc)),
                  pl.BlockSpec((ATTN_TILE, HEAD_DIM), cur(kc)),
                  pl.BlockSpec((ATTN_TILE, HEAD_DIM), cur(vc)),
                  pl.BlockSpec((span, HEAD_DIM), prev(kc)),
                  pl.BlockSpec((span, HEAD_DIM), prev(vc)),
                  pl.BlockSpec((None, None, BAND, 2 * BAND), lambda b, i, h, g=g: (g, h, 0, 0))]
        args += [proj] * 5 + [bias_p]
    return pl.pallas_call(
        _prompt_attn_kernel,
        grid=(batch, ntile, H),
        in_specs=specs,
        out_specs=pl.BlockSpec((ATTN_TILE, HEAD_DIM), lambda b, i, h: (b * ntile + i, h)),
        out_shape=jax.ShapeDtypeStruct((batch * seq, WIDTH_A_OUT), F32),
        scratch_shapes=[pltpu.VMEM((N_GROUPS, ATTN_TILE, HEAD_DIM), F32),
                        pltpu.VMEM((2 * N_GROUPS, ATTN_TILE, 1), F32)],
        compiler_params=_cparams(("parallel", "arbitrary", "arbitrary")),
        name="prompt_attn",
    )(*args)


def _mem_attn_kernel(q_ref, k_ref, v_ref, o_ref):
    for h in range(MEM_HEADS):
        hs = slice(h * HEAD_DIM, (h + 1) * HEAD_DIM)
        s = _dot_nt(q_ref[:, hs], k_ref[:, hs]) * SCALE
        m = jnp.max(s, axis=-1, keepdims=True)
        p = jnp.exp(s - m)
        l = jnp.sum(p, axis=-1, keepdims=True)
        o_ref[:, hs] = _dot(p, v_ref[:, hs]) / l


def _mem_attn(proj, qcol, mk, mv, layer, batch, seq, tm):
    steps = seq // tm
    return pl.pallas_call(
        _mem_attn_kernel,
        grid=(batch, steps),
        in_specs=[pl.BlockSpec((tm, WIDTH_MEM), lambda b, i: (b * steps + i, qcol)),
                  pl.BlockSpec((None, N_MEM, WIDTH_MEM), lambda b, i: (layer, b, 0)),
                  pl.BlockSpec((None, N_MEM, WIDTH_MEM), lambda b, i: (layer, b, 0))],
        out_specs=pl.BlockSpec((tm, WIDTH_MEM), lambda b, i: (b * steps + i, 0)),
        out_shape=jax.ShapeDtypeStruct((batch * seq, WIDTH_MEM), F32),
        compiler_params=_cparams(("parallel", "parallel")),
        name="mem_attn",
    )(proj, mk, mv)


def _out_proj_kernel(a1_ref, a2_ref, z1_ref, z2_ref, x_ref, w1_ref, w2_ref, o_ref):
    g1 = (a1_ref[...] * _silu(z1_ref[...])).astype(BF16)
    g2 = (a2_ref[...] * _silu(z2_ref[...])).astype(BF16)
    y = _mm(g1, w1_ref[...])
    y = y + _mm(g2, w2_ref[...])
    o_ref[...] = x_ref[...] + y


def _out_proj(a1, a2, proj, z1col, z2col, x, w1, w2, tm):
    m, d = x.shape
    k1, k2 = a1.shape[1], a2.shape[1]
    return pl.pallas_call(
        _out_proj_kernel,
        grid=(m // tm,),
        in_specs=[pl.BlockSpec((tm, k1), lambda i: (i, 0)),
                  pl.BlockSpec((tm, k2), lambda i: (i, 0)),
                  pl.BlockSpec((tm, k1), lambda i: (i, z1col)),
                  pl.BlockSpec((tm, k2), lambda i: (i, z2col)),
                  pl.BlockSpec((tm, d), lambda i: (i, 0)),
                  pl.BlockSpec((k1, d), lambda i: (0, 0)),
                  pl.BlockSpec((k2, d), lambda i: (0, 0))],
        out_specs=pl.BlockSpec((tm, d), lambda i: (i, 0)),
        out_shape=jax.ShapeDtypeStruct((m, d), F32),
        input_output_aliases={4: 0},
        compiler_params=_cparams(("parallel",)),
        name="out_proj",
    )(a1, a2, proj, proj, x, w1, w2)


def _row_attend(q, ks, vs, bias_col, k0, v0, b0):
    s = jnp.sum(ks * q, axis=-1, keepdims=True) * SCALE + bias_col
    s0 = jnp.sum(k0 * q, axis=-1, keepdims=True) * SCALE + b0
    m = jnp.maximum(jnp.max(s, axis=0, keepdims=True), s0)
    p = jnp.exp(s - m)
    p0 = jnp.exp(s0 - m)
    l = jnp.sum(p, axis=0, keepdims=True) + p0
    acc = jnp.sum(p * vs, axis=0, keepdims=True) + p0 * v0
    return acc, m, l


def _shift_copy(src_ref, dst_ref, drop, chunk=256):
    total = src_ref.shape[0] - drop
    steps = -(-total // chunk)

    def body(i, carry):
        start = pl.multiple_of(jnp.minimum(i * chunk, total - chunk), 8)
        dst_ref[pl.ds(start, chunk), :] = src_ref[pl.ds(start + drop, chunk), :]
        return carry

    lax.fori_loop(0, steps, body, 0)


def _sample_win_kernel(*refs, n_new, n_alias):
    H = HEADS_PER_GROUP
    new_ref, bias_ref = refs[0], refs[7]
    cin = ((refs[1], refs[2]), (refs[3], refs[4]), (refs[5], refs[6]))
    outs = refs[8 + n_alias:]
    cout = ((outs[0], outs[1]), (outs[2], outs[3]), (outs[4], outs[5]))
    o_ref, kcat_ref, vcat_ref = outs[6], outs[7], outs[8]
    qb, kb, vb = 0, N_GROUPS * H, 2 * N_GROUPS * H

    for g in range(N_GROUPS):
        for (src, dst, base) in ((cin[g][0], cout[g][0], kb), (cin[g][1], cout[g][1], vb)):
            rows = src.shape[0]
            _shift_copy(src, dst, n_new * H)
            for t in range(n_new):
                dst[rows - (n_new - t) * H: rows - (n_new - t - 1) * H, :] = \
                    new_ref[t, base + g * H: base + (g + 1) * H, :]

    rows0 = cin[0][0].shape[0]
    kcat_ref[0:rows0, :] = cin[0][0][...]
    vcat_ref[0:rows0, :] = cin[0][1][...]
    for t in range(n_new):
        kcat_ref[rows0 + t * H: rows0 + (t + 1) * H, :] = new_ref[t, kb:kb + H, :]
        vcat_ref[rows0 + t * H: rows0 + (t + 1) * H, :] = new_ref[t, vb:vb + H, :]

    for t in range(n_new):
        for h in range(H):
            parts = []
            for g, (win, dil) in enumerate(DIL_GROUPS):
                c = g * H + h
                q = new_ref[t, qb + c: qb + c + 1, :]
                k_self = new_ref[t, kb + c: kb + c + 1, :]
                v_self = new_ref[t, vb + c: vb + c + 1, :]
                ksrc, vsrc = (kcat_ref, vcat_ref) if g == 0 else cin[g]
                sl = pl.ds(t * H + h, BAND, stride=dil * H)
                parts.append(_row_attend(q, ksrc[sl, :], vsrc[sl, :],
                                         bias_ref[g, 0:BAND, c:c + 1],
                                         k_self, v_self, bias_ref[g, BAND:BAND + 1, c:c + 1]))
            m = jnp.maximum(jnp.maximum(parts[0][1], parts[1][1]), parts[2][1])
            num = jnp.zeros((1, HEAD_DIM), F32)
            den = jnp.zeros((1, 1), F32)
            for acc, mg, lg in parts:
                w = jnp.exp(mg - m)
                num = num + w * acc
                den = den + w * lg
            o_ref[t, h:h + 1, :] = num / den


def _sample_win(proj_s, caches, layer, prev_out, bias_s, batch, n_new):
    H = HEADS_PER_GROUP
    for g, (win, dil) in enumerate(DIL_GROUPS):
        assert caches[2 * g].shape[2] == win * H and win // dil == BAND and n_new * H % 8 == 0
    new = proj_s.reshape(batch, n_new, IN_A // 128, 128)
    cspecs = [pl.BlockSpec((None, None) + c.shape[2:], lambda b: (layer, b, 0, 0)) for c in caches]
    rows0 = caches[0].shape[2]
    alias_args = list(prev_out) if prev_out is not None else []
    n_alias = len(alias_args)
    n_in = 8
    outs = pl.pallas_call(
        functools.partial(_sample_win_kernel, n_new=n_new, n_alias=n_alias),
        grid=(batch,),
        in_specs=[pl.BlockSpec((None, n_new, IN_A // 128, 128), lambda b: (b, 0, 0, 0))] + cspecs
                 + [pl.BlockSpec(bias_s.shape, lambda b: (0, 0, 0))]
                 + [pl.BlockSpec(memory_space=pl.ANY)] * n_alias,
        out_specs=cspecs + [pl.BlockSpec((None, n_new, H, HEAD_DIM), lambda b: (b, 0, 0, 0))],
        out_shape=[jax.ShapeDtypeStruct(c.shape, F32) for c in caches]
                  + [jax.ShapeDtypeStruct((batch, n_new, H, HEAD_DIM), F32)],
        scratch_shapes=[pltpu.VMEM((rows0 + n_new * H, 128), F32),
                        pltpu.VMEM((rows0 + n_new * H, 128), F32)],
        input_output_aliases={n_in + k: k for k in range(n_alias)},
        compiler_params=_cparams(("parallel",)),
        name="sample_win_attn",
    )(new, *caches, bias_s, *alias_args)
    return outs[:6], outs[6].reshape(batch * n_new, WIDTH_A_OUT)


def _sample_mem_kernel(q_ref, k_ref, v_ref, o_ref, *, n_new, qblk):
    pad = jnp.zeros((8 - n_new, HEAD_DIM), F32)
    for h in range(MEM_HEADS):
        q = jnp.concatenate([q_ref[t, qblk + h: qblk + h + 1, :] for t in range(n_new)] + [pad], axis=0)
        sl = pl.ds(h, N_MEM, stride=MEM_HEADS)
        s = _dot_nt(q, k_ref[sl, :]) * SCALE
        m = jnp.max(s, axis=-1, keepdims=True)
        p = jnp.exp(s - m)
        l = jnp.sum(p, axis=-1, keepdims=True)
        o = _dot(p, v_ref[sl, :]) / l
        for t in range(n_new):
            o_ref[t, h:h + 1, :] = o[t:t + 1, :]


def _sample_mem(proj_s, qcol512, mem_k, mem_v, layer, batch, n_new):
    ncol = proj_s.shape[1] // 128
    new = proj_s.reshape(batch, n_new, ncol, 128)
    mspec = pl.BlockSpec((None, None, N_MEM * MEM_HEADS, HEAD_DIM), lambda b: (layer, b, 0, 0))
    out = pl.pallas_call(
        functools.partial(_sample_mem_kernel, n_new=n_new, qblk=qcol512 * 4),
        grid=(batch,),
        in_specs=[pl.BlockSpec((None, n_new, ncol, 128), lambda b: (b, 0, 0, 0)), mspec, mspec],
        out_specs=pl.BlockSpec((None, n_new, MEM_HEADS, HEAD_DIM), lambda b: (b, 0, 0, 0)),
        out_shape=jax.ShapeDtypeStruct((batch, n_new, MEM_HEADS, HEAD_DIM), F32),
        compiler_params=_cparams(("parallel",)),
        name="sample_mem_attn",
    )(new, mem_k, mem_v)
    return out.reshape(batch * n_new, WIDTH_MEM)


def _gate_terms(ba, alog_row, dtb_row):
    beta = _sigmoid(ba)
    g = -jnp.exp(alog_row) * _softplus(ba + dtb_row)
    return beta, g


def _delta_prompt_kernel(x_ref, halo_ref, ba_ref, cw_ref, alog_ref, dtb_ref, onw_ref,
                         o_ref, s_out_ref, xs_ref, s_ref):
    C = BLOCK_ROWS
    NH = N_HEADS_B
    it = pl.program_id(1)

    @pl.when(it == 0)
    def _():
        s_ref[...] = jnp.zeros(s_ref.shape, F32)
        xs_ref[0:8, :] = jnp.zeros((8, xs_ref.shape[1]), F32)

    @pl.when(it > 0)
    def _():
        xs_ref[0:8, :] = halo_ref[...]

    xs_ref[8:8 + C, :] = x_ref[...]

    def conv_block(c0):
        cs = slice(c0, c0 + HEAD_DIM)
        acc = xs_ref[pl.ds(8 - (CONV_W - 1), C), cs] * cw_ref[0:1, cs]
        for wi in range(1, CONV_W):
            acc = acc + xs_ref[pl.ds(8 - (CONV_W - 1) + wi, C), cs] * cw_ref[wi:wi + 1, cs]
        return _silu(acc)

    beta, g = _gate_terms(ba_ref[...], alog_ref[...], dtb_ref[...])
    row = lax.broadcasted_iota(jnp.int32, (C, C), 0)
    col = lax.broadcasted_iota(jnp.int32, (C, C), 1)
    tri = row >= col
    strict = row > col
    ones_tri = jnp.where(tri, 1.0, 0.0).astype(BF16)
    gh, gm, gl = _split3(g)
    gc = _mm(ones_tri, gh) + (_mm(ones_tri, gm) + _mm(ones_tri, gl))
    gc_t = gc.T
    eye = jnp.where(row == col, 1.0, 0.0)

    q, k, v, kb, egc, decay, b_col, gc_col = [], [], [], [], [], [], [], []
    m_hi, m_lo, tinv = [], [], []
    for h in range(NH):
        q.append(_l2norm(conv_block(h * HEAD_DIM)) * SCALE)
        k.append(_l2norm(conv_block(WIDTH_B + h * HEAD_DIM)))
        v.append(conv_block(2 * WIDTH_B + h * HEAD_DIM))
        b_col.append(beta[:, h:h + 1])
        gc_col.append(gc[:, 8 + h: 9 + h])
        gc_row = gc_t[8 + h: 9 + h, :]
        decay.append(jnp.where(tri, jnp.exp(jnp.where(tri, gc_col[h] - gc_row, 0.0)), 0.0))
        kb.append(k[h] * b_col[h])
        egc.append(jnp.exp(gc_col[h]))
        lmat = jnp.where(strict, _dot_nt(kb[h], k[h]) * decay[h], 0.0)
        hi, lo = _split2(-lmat)
        m_hi.append(hi)
        m_lo.append(lo)
        tinv.append(eye - lmat)

    for _ in range(int(math.log2(C)) - 1):
        for h in range(NH):
            sq = _mm(m_hi[h], m_hi[h]) + (_mm(m_hi[h], m_lo[h]) + _mm(m_lo[h], m_hi[h]))
            m_hi[h], m_lo[h] = _split2(sq)
        for h in range(NH):
            t_hi, t_lo = _split2(tinv[h])
            tinv[h] = tinv[h] + (_mm(t_hi, m_hi[h]) + (_mm(t_hi, m_lo[h]) + _mm(t_lo, m_hi[h])))

    uw = [_dot(tinv[h], jnp.concatenate([v[h] * b_col[h], kb[h] * egc[h]], axis=1)) for h in range(NH)]
    aqk = [_dot_nt(q[h], k[h]) * decay[h] for h in range(NH)]
    for h in range(NH):
        hs = slice(h * HEAD_DIM, (h + 1) * HEAD_DIM)
        g_last = gc_col[h][C - 1:C, :]
        kdec = k[h] * jnp.exp(g_last - gc_col[h])
        s_prev = s_ref[h]
        rs = _dot(jnp.concatenate([uw[h][:, HEAD_DIM:], q[h] * egc[h]], axis=0), s_prev)
        v_new = uw[h][:, :HEAD_DIM] - rs[0:C]
        o = rs[C:2 * C] + _dot(aqk[h], v_new)
        s_ref[h] = s_prev * jnp.exp(g_last) + _dot(kdec.T, v_new)
        ms = jnp.mean(o * o, axis=-1, keepdims=True)
        o_ref[:, hs] = o * lax.rsqrt(ms + EPS) * onw_ref[...]

    @pl.when(it == pl.num_programs(1) - 1)
    def _():
        s_out_ref[...] = s_ref[...]


def _delta_prompt(proj, conv_w, alog_row, dtb_row, onw, batch, seq):
    C = BLOCK_ROWS
    steps = seq // C
    wq = 3 * WIDTH_B
    row1 = lambda shape: pl.BlockSpec(shape, lambda b, i: (0, 0))
    return pl.pallas_call(
        _delta_prompt_kernel,
        grid=(batch, steps),
        in_specs=[pl.BlockSpec((C, wq), lambda b, i: (b * steps + i, 0)),
                  pl.BlockSpec((8, wq), lambda b, i: (jnp.maximum((b * steps + i) * (C // 8) - 1, 0), 0)),
                  pl.BlockSpec((C, 128), lambda b, i: (b * steps + i, BA_COL_BLOCK)),
                  row1((CONV_W, wq)), row1((1, 128)), row1((1, 128)), row1((1, HEAD_DIM))],
        out_specs=[pl.BlockSpec((C, WIDTH_B), lambda b, i: (b * steps + i, 0)),
                   pl.BlockSpec((None, N_HEADS_B, HEAD_DIM, HEAD_DIM), lambda b, i: (b, 0, 0, 0))],
        out_shape=[jax.ShapeDtypeStruct((batch * seq, WIDTH_B), F32),
                   jax.ShapeDtypeStruct((batch, N_HEADS_B, HEAD_DIM, HEAD_DIM), F32)],
        scratch_shapes=[pltpu.VMEM((8 + C, wq), F32),
                        pltpu.VMEM((N_HEADS_B, HEAD_DIM, HEAD_DIM), F32)],
        compiler_params=_cparams(("parallel", "arbitrary")),
        name="delta_prompt",
    )(proj, proj, proj, conv_w, alog_row, dtb_row, onw)


def _delta_sample_kernel(*refs, n_new, n_alias):
    NH = N_HEADS_B
    x_ref, cs_ref, s_in_ref, cw_ref, alog_ref, dtb_ref, onw_ref = refs[:7]
    o_ref, s_out_ref, st_ref, rs_ref, kd_ref, vn_ref = refs[7 + n_alias:]
    qkv_rows = 3 * NH

    xp = [cs_ref[i] for i in range(CONV_W - 1)] + [x_ref[t, 0:qkv_rows, :] for t in range(n_new)]
    ba = jnp.concatenate([x_ref[t, BA_COL_BLOCK:BA_COL_BLOCK + 1, :] for t in range(n_new)]
                         + [jnp.zeros((HEAD_DIM - n_new, 128), F32)], axis=0)
    ba_t = ba.T
    alog_col = alog_ref[:, 0:1]
    dtb_col = dtb_ref[:, 0:1]

    q, k, v, beta, gc = [], [], [], [], []
    run = jnp.zeros((NH, 1), F32)
    for t in range(n_new):
        acc = xp[t] * cw_ref[0]
        for wi in range(1, CONV_W):
            acc = acc + xp[t + wi] * cw_ref[wi]
        conv = _silu(acc)
        q.append(_l2norm(conv[0:NH]) * SCALE)
        k.append(_l2norm(conv[NH:2 * NH]))
        v.append(conv[2 * NH:3 * NH])
        beta.append(_sigmoid(ba_t[0:NH, t:t + 1]))
        run = run - jnp.exp(alog_col) * _softplus(ba_t[NH:2 * NH, t:t + 1] + dtb_col)
        gc.append(run)
    egc = [jnp.exp(x) for x in gc]
    dec = {(i, j): jnp.exp(gc[i] - gc[j]) for i in range(n_new) for j in range(i)}
    lsum = lambda a, b: jnp.sum(a * b, axis=-1, keepdims=True)

    u, w = [], []
    for i in range(n_new):
        ui = v[i] * beta[i]
        wi_ = k[i] * (beta[i] * egc[i])
        for j in range(i):
            lij = beta[i] * lsum(k[i], k[j]) * dec[i, j]
            ui = ui - lij * u[j]
            wi_ = wi_ - lij * w[j]
        u.append(ui)
        w.append(wi_)

    for i in range(n_new):
        st_ref[i * NH:(i + 1) * NH, :] = w[i]
        st_ref[(n_new + i) * NH:(n_new + i + 1) * NH, :] = q[i] * egc[i]
    for h in range(NH):
        rs_ref[h * 8:(h + 1) * 8, :] = _dot(st_ref[pl.ds(h, 2 * n_new, stride=NH), :], s_in_ref[h])

    vn, g_last = [], gc[n_new - 1]
    for i in range(n_new):
        vn.append(u[i] - rs_ref[pl.ds(i, NH, stride=8), :])
    for i in range(n_new):
        o = rs_ref[pl.ds(n_new + i, NH, stride=8), :] + lsum(q[i], k[i]) * vn[i]
        for j in range(i):
            o = o + (lsum(q[i], k[j]) * dec[i, j]) * vn[j]
        ms = jnp.mean(o * o, axis=-1, keepdims=True)
        o_ref[i] = o * lax.rsqrt(ms + EPS) * onw_ref[...]

    kd_ref[...] = jnp.zeros(kd_ref.shape, F32)
    vn_ref[...] = jnp.zeros(vn_ref.shape, F32)
    for i in range(n_new):
        kd_ref[i * NH:(i + 1) * NH, :] = k[i] * jnp.exp(g_last - gc[i])
        vn_ref[i * NH:(i + 1) * NH, :] = vn[i]
    eg_last = jnp.exp(g_last)
    zpad = jnp.zeros((HEAD_DIM - 8, HEAD_DIM), F32)
    for h in range(NH):
        kd = jnp.concatenate([kd_ref[pl.ds(h, 8, stride=NH), :], zpad], axis=0)
        vh = jnp.concatenate([vn_ref[pl.ds(h, 8, stride=NH), :], zpad], axis=0)
        s_out_ref[h] = s_in_ref[h] * eg_last[h:h + 1, :] + _dot(kd.T, vh)


def _delta_sample(proj_s, conv_state, delta_state, layer, prev_out, conv_w, a_log, dt_bias, onw, batch, n_new):
    assert n_new == 4 and N_HEADS_B == 8
    ncol = IN_B_PAD // 128
    new = proj_s.reshape(batch, n_new, ncol, 128)
    cw = conv_w.reshape(CONV_W, 3 * N_HEADS_B, HEAD_DIM)
    alog = jnp.broadcast_to(a_log[:, None], (N_HEADS_B, 128))
    dtb = jnp.broadcast_to(dt_bias[:, None], (N_HEADS_B, 128))
    sspec = pl.BlockSpec((None, None, N_HEADS_B, HEAD_DIM, HEAD_DIM), lambda b: (layer, b, 0, 0, 0))
    alias_args = [prev_out] if prev_out is not None else []
    n_alias = len(alias_args)
    o, s = pl.pallas_call(
        functools.partial(_delta_sample_kernel, n_new=n_new, n_alias=n_alias),
        grid=(batch,),
        in_specs=[pl.BlockSpec((None, n_new, ncol, 128), lambda b: (b, 0, 0, 0)),
                  pl.BlockSpec((None, None, CONV_W - 1, 3 * N_HEADS_B, HEAD_DIM), lambda b: (layer, b, 0, 0, 0)),
                  sspec,
                  pl.BlockSpec(cw.shape, lambda b: (0, 0, 0)),
                  pl.BlockSpec((N_HEADS_B, 128), lambda b: (0, 0)),
                  pl.BlockSpec((N_HEADS_B, 128), lambda b: (0, 0)),
                  pl.BlockSpec((1, HEAD_DIM), lambda b: (0, 0))]
                 + [pl.BlockSpec(memory_space=pl.ANY)] * n_alias,
        out_specs=[pl.BlockSpec((None, n_new, N_HEADS_B, HEAD_DIM), lambda b: (b, 0, 0, 0)), sspec],
        out_shape=[jax.ShapeDtypeStruct((batch, n_new, N_HEADS_B, HEAD_DIM), F32),
                   jax.ShapeDtypeStruct(delta_state.shape, F32)],
        scratch_shapes=[pltpu.VMEM((2 * n_new * N_HEADS_B, HEAD_DIM), F32),
                        pltpu.VMEM((8 * N_HEADS_B, HEAD_DIM), F32),
                        pltpu.VMEM((8 * N_HEADS_B, HEAD_DIM), F32),
                        pltpu.VMEM((8 * N_HEADS_B, HEAD_DIM), F32)],
        input_output_aliases={7 + k: 1 + k for k in range(n_alias)},
        compiler_params=_cparams(("parallel",)),
        name="delta_sample",
    )(new, conv_state, delta_state, cw, alog, dtb, onw, *alias_args)
    return o.reshape(batch * n_new, WIDTH_B), s


def _reorder_w_in_b(w):
    q0 = 3 * WIDTH_B
    b0, a0, m0, z0 = q0, q0 + N_HEADS_B, q0 + 2 * N_HEADS_B, q0 + 2 * N_HEADS_B + WIDTH_MEM
    parts = [w[:, :q0], w[:, z0:], w[:, m0:z0], w[:, b0:a0], w[:, a0:m0]]
    out = jnp.concatenate(parts, axis=1)
    return jnp.pad(out, ((0, 0), (0, IN_B_PAD - out.shape[1])))


def _lane_row(vals, offset):
    return jnp.zeros((1, 128), F32).at[0, offset:offset + vals.shape[0]].set(vals)


def kernel(x_prompt, x_sample, cache_win_k0, cache_win_v0, cache_win_k1, cache_win_v1, cache_win_k2, cache_win_v2, state_conv, state_delta, cache_mem_k, cache_mem_v, mem_prompt, norm_w, final_norm_w, rel_bias, w_in_a, w_out_a, w_in_b, conv_w, a_log, dt_bias, o_norm_w, w_out_b, w_mem_kv):
    bp, seq, d = x_prompt.shape
    bs, n_new, _ = x_sample.shape
    depth = norm_w.shape[0]
    H = HEADS_PER_GROUP
    tm_p = 1024

    xp = x_prompt.reshape(bp * seq, d)
    xs = x_sample.reshape(bs * n_new, d)

    mk_all, mv_all = _mem_kv(mem_prompt.reshape(bp * N_MEM, d), w_mem_kv.astype(BF16))
    bias_p = _bias_prompt(rel_bias)
    bias_s = _bias_sample(jnp.pad(rel_bias, ((0, 0), (0, 128 - rel_bias.shape[1]))))
    mem_k_s = cache_mem_k.reshape(depth, bs, N_MEM * MEM_HEADS, HEAD_DIM)
    mem_v_s = cache_mem_v.reshape(depth, bs, N_MEM * MEM_HEADS, HEAD_DIM)
    caches_in = (cache_win_k0, cache_win_v0, cache_win_k1, cache_win_v1, cache_win_k2, cache_win_v2)
    caches_flat = [c.reshape(c.shape[0], bs, c.shape[2] * H, HEAD_DIM) for c in caches_in]
    conv_state = state_conv.reshape(state_conv.shape[0], bs, CONV_W - 1, 3 * N_HEADS_B, HEAD_DIM)

    p_win = [[] for _ in range(2 * N_GROUPS)]
    p_conv, p_delta, s_conv = [], [], []
    s_win, s_delta = None, None

    for i in range(depth):
        li = i // 2
        nw = norm_w[i].reshape(1, d)
        if i % 2 == 0:
            w_in = w_in_a[li].astype(BF16)
            w_out = w_out_a[li].astype(BF16)
            proj_p = _norm_proj(xp, nw, w_in, tm_p, 2048)
            proj_s = _norm_proj(xs, nw, w_in, bs * n_new, 2048)
            p3 = proj_p.reshape(bp, seq, IN_A)
            for g, (win, dil) in enumerate(DIL_GROUPS):
                keep = min(win, seq)
                kcol = WIDTH_A_QKV + g * WIDTH_A_OUT
                vcol = 2 * WIDTH_A_QKV + g * WIDTH_A_OUT
                p_win[2 * g].append(p3[:, seq - keep:, kcol:kcol + WIDTH_A_OUT].reshape(bp, keep, H, HEAD_DIM))
                p_win[2 * g + 1].append(p3[:, seq - keep:, vcol:vcol + WIDTH_A_OUT].reshape(bp, keep, H, HEAD_DIM))
            o_p = _prompt_attn(proj_p, bias_p, bp, seq)
            om_p = _mem_attn(proj_p, 9, mk_all, mv_all, i, bp, seq, 512)
            xp = _out_proj(o_p, om_p, proj_p, 10, 11, xp, w_out[:WIDTH_A_OUT], w_out[WIDTH_A_OUT:], 512)
            s_win, o_s = _sample_win(proj_s, caches_flat, li, s_win, bias_s, bs, n_new)
            om_s = _sample_mem(proj_s, 9, mem_k_s, mem_v_s, i, bs, n_new)
            xs = _out_proj(o_s, om_s, proj_s, 10, 11, xs, w_out[:WIDTH_A_OUT], w_out[WIDTH_A_OUT:], bs * n_new)
        else:
            w_in = _reorder_w_in_b(w_in_b[li]).astype(BF16)
            w_out = w_out_b[li].astype(BF16)
            onw = o_norm_w[li].reshape(1, HEAD_DIM)
            proj_p = _norm_proj(xp, nw, w_in, tm_p, 1792)
            proj_s = _norm_proj(xs, nw, w_in, bs * n_new, 1792)
            wq = 3 * WIDTH_B
            o_p, s_p = _delta_prompt(proj_p, conv_w[li], _lane_row(a_log[li], N_HEADS_B),
                                     _lane_row(dt_bias[li], N_HEADS_B), onw, bp, seq)
            p_delta.append(s_p)
            p_conv.append(proj_p.reshape(bp, seq, IN_B_PAD)[:, seq - (CONV_W - 1):, :wq])
            om_p = _mem_attn(proj_p, 9, mk_all, mv_all, i, bp, seq, 512)
            xp = _out_proj(o_p, om_p, proj_p, 3, 8, xp, w_out[:WIDTH_B], w_out[WIDTH_B:], 512)
            o_s, s_delta = _delta_sample(proj_s, conv_state, state_delta, li, s_delta, conv_w[li],
                                         a_log[li], dt_bias[li], onw, bs, n_new)
            xcat = jnp.concatenate([state_conv[li], proj_s.reshape(bs, n_new, IN_B_PAD)[:, :, :wq]], axis=1)
            s_conv.append(xcat[:, n_new:])
            om_s = _sample_mem(proj_s, 9, mem_k_s, mem_v_s, i, bs, n_new)
            xs = _out_proj(o_s, om_s, proj_s, 3, 8, xs, w_out[:WIDTH_B], w_out[WIDTH_B:], bs * n_new)

    fnw = final_norm_w.reshape(1, d)
    y_prompt = _final_norm(xp, fnw, tm_p).reshape(bp, seq, d)
    y_sample = _final_norm(xs, fnw, bs * n_new).reshape(bs, n_new, d)
    p_mk = mk_all.reshape(depth, bp, N_MEM, MEM_HEADS, HEAD_DIM)
    p_mv = mv_all.reshape(depth, bp, N_MEM, MEM_HEADS, HEAD_DIM)
    s_win = [s_win[n].reshape(caches_in[n].shape) for n in range(2 * N_GROUPS)]
    return (y_prompt, y_sample,
            jnp.stack(p_win[0]), jnp.stack(p_win[1]), jnp.stack(p_win[2]),
            jnp.stack(p_win[3]), jnp.stack(p_win[4]), jnp.stack(p_win[5]),
            jnp.stack(p_conv), jnp.stack(p_delta), p_mk, p_mv,
            s_win[0], s_win[1], s_win[2], s_win[3], s_win[4], s_win[5],
            jnp.stack(s_conv), s_delta)
```

```python
import functools
import math

import jax
import jax.numpy as jnp
import numpy as np
from jax import lax
from jax.experimental import pallas as pl
from jax.experimental.pallas import tpu as pltpu

F32 = jnp.float32
BF16 = jnp.bfloat16

D_MODEL = 1024
HEAD_DIM = 128
DIL_GROUPS = ((128, 1), (512, 4), (2048, 16))
N_GROUPS = len(DIL_GROUPS)
HEADS_PER_GROUP = 4
WIDTH_A_QKV = N_GROUPS * HEADS_PER_GROUP * HEAD_DIM
WIDTH_A_OUT = HEADS_PER_GROUP * HEAD_DIM
N_MEM = 256
MEM_HEADS = 4
WIDTH_MEM = MEM_HEADS * HEAD_DIM
N_HEADS_B = 8
WIDTH_B = N_HEADS_B * HEAD_DIM
CONV_W = 4
N_BUCKETS = 32
MAX_EXACT = N_BUCKETS // 2
MAX_DIST = 2048
EPS = 1e-6
NEG = -1e30
SCALE = HEAD_DIM ** -0.5
BAND = 128
BLOCK_ROWS = 128
ATTN_TILE = BAND * max(d for _, d in DIL_GROUPS)
IN_A = 3 * WIDTH_A_QKV + WIDTH_MEM + WIDTH_A_OUT + WIDTH_MEM
IN_B_PAD = 5376
BA_COL_BLOCK = (3 * WIDTH_B + WIDTH_B + WIDTH_MEM + WIDTH_MEM) // 128
VMEM_LIMIT = 48 * 1024 * 1024
FUSED_VMEM_LIMIT = 58 * 1024 * 1024


def _cparams(sem):
    return pltpu.CompilerParams(dimension_semantics=sem, vmem_limit_bytes=VMEM_LIMIT)


def _sigmoid(x):
    return 1.0 / (1.0 + jnp.exp(-x))


def _silu(x):
    return x * _sigmoid(x)


def _softplus(x):
    return jnp.maximum(x, 0.0) + jnp.log(1.0 + jnp.exp(-jnp.abs(x)))


def _mm(a, b):
    return jnp.dot(a, b, preferred_element_type=F32)


def _dot(a, b):
    return _mm(a.astype(BF16), b.astype(BF16))


def _dot_nt(a, b):
    return lax.dot_general(a.astype(BF16), b.astype(BF16), (((1,), (1,)), ((), ())),
                           preferred_element_type=F32)


def _split2(a):
    hi = a.astype(BF16)
    lo = (a - hi.astype(F32)).astype(BF16)
    return hi, lo


def _split3(a):
    hi = a.astype(BF16)
    r1 = a - hi.astype(F32)
    mid = r1.astype(BF16)
    lo = (r1 - mid.astype(F32)).astype(BF16)
    return hi, mid, lo


def _l2norm(x):
    return x * lax.rsqrt(jnp.sum(x * x, axis=-1, keepdims=True) + EPS)


def _norm_proj_kernel(x_ref, nw_ref, w_ref, o_ref, h_ref):
    @pl.when(pl.program_id(1) == 0)
    def _():
        x = x_ref[...]
        ms = jnp.mean(x * x, axis=-1, keepdims=True)
        h_ref[...] = (x * lax.rsqrt(ms + EPS) * nw_ref[...]).astype(BF16)

    o_ref[...] = _mm(h_ref[...], w_ref[...])


def _norm_proj(x, nw, w, tm, tn):
    m, d = x.shape
    n = w.shape[1]
    return pl.pallas_call(
        _norm_proj_kernel,
        grid=(m // tm, n // tn),
        in_specs=[pl.BlockSpec((tm, d), lambda i, j: (i, 0)),
                  pl.BlockSpec((1, d), lambda i, j: (0, 0)),
                  pl.BlockSpec((d, tn), lambda i, j: (0, j))],
        out_specs=pl.BlockSpec((tm, tn), lambda i, j: (i, j)),
        out_shape=jax.ShapeDtypeStruct((m, n), F32),
        scratch_shapes=[pltpu.VMEM((tm, d), BF16)],
        compiler_params=_cparams(("parallel", "arbitrary")),
        name="norm_proj",
    )(x, nw, w)


def _final_norm_kernel(x_ref, nw_ref, o_ref):
    x = x_ref[...]
    ms = jnp.mean(x * x, axis=-1, keepdims=True)
    o_ref[...] = x * lax.rsqrt(ms + EPS) * nw_ref[...]


def _final_norm(x, nw, tm):
    m, d = x.shape
    return pl.pallas_call(
        _final_norm_kernel,
        grid=(m // tm,),
        in_specs=[pl.BlockSpec((tm, d), lambda i: (i, 0)),
                  pl.BlockSpec((1, d), lambda i: (0, 0))],
        out_specs=pl.BlockSpec((tm, d), lambda i: (i, 0)),
        out_shape=jax.ShapeDtypeStruct((m, d), F32),
        compiler_params=_cparams(("parallel",)),
        name="final_norm",
    )(x, nw)


def _mem_kv_kernel(x_ref, w_ref, k_ref, v_ref):
    r = _mm(x_ref[...].astype(BF16), w_ref[...])
    k_ref[...] = r[:, :WIDTH_MEM]
    v_ref[...] = r[:, WIDTH_MEM:]


def _mem_kv(mem, w):
    m, d = mem.shape
    depth = w.shape[0]
    out = jax.ShapeDtypeStruct((depth, m, WIDTH_MEM), F32)
    return pl.pallas_call(
        _mem_kv_kernel,
        grid=(depth,),
        in_specs=[pl.BlockSpec((m, d), lambda i: (0, 0)),
                  pl.BlockSpec((None, d, 2 * WIDTH_MEM), lambda i: (i, 0, 0))],
        out_specs=[pl.BlockSpec((None, m, WIDTH_MEM), lambda i: (i, 0, 0)),
                   pl.BlockSpec((None, m, WIDTH_MEM), lambda i: (i, 0, 0))],
        out_shape=[out, out],
        compiler_params=_cparams(("parallel",)),
        name="mem_kv",
    )(mem, w)


def _rel_bucket_np(dist):
    dist = np.asarray(dist, np.int64)
    df = np.maximum(dist, 1).astype(np.float32)
    large = MAX_EXACT + (np.log(df / np.float32(MAX_EXACT)) / np.float32(math.log(MAX_DIST / MAX_EXACT))
                         * np.float32(N_BUCKETS - MAX_EXACT)).astype(np.int32)
    return np.where(dist < MAX_EXACT, dist, np.minimum(large, N_BUCKETS - 1)).astype(np.int32)


def _bias_prompt_kernel(bkt_ref, tab_ref, o_ref):
    g = pl.program_id(0)
    bkt = bkt_ref[...]
    accs = [jnp.zeros(bkt.shape, F32) for _ in range(HEADS_PER_GROUP)]
    for b in range(N_BUCKETS):
        hit = bkt == b
        for h in range(HEADS_PER_GROUP):
            accs[h] = jnp.where(hit, tab_ref[b, g * HEADS_PER_GROUP + h], accs[h])
    for h in range(HEADS_PER_GROUP):
        o_ref[h] = accs[h]


def _bias_prompt(rel_bias):
    i = np.arange(BAND)[:, None]
    j = np.arange(2 * BAND)[None, :]
    dsub = np.maximum(i + BAND - j, 0)
    bkt = np.stack([_rel_bucket_np(dsub * dil) for _, dil in DIL_GROUPS])
    return pl.pallas_call(
        _bias_prompt_kernel,
        grid=(N_GROUPS,),
        in_specs=[pl.BlockSpec((None, BAND, 2 * BAND), lambda g: (g, 0, 0)),
                  pl.BlockSpec(memory_space=pltpu.SMEM)],
        out_specs=pl.BlockSpec((None, HEADS_PER_GROUP, BAND, 2 * BAND), lambda g: (g, 0, 0, 0)),
        out_shape=jax.ShapeDtypeStruct((N_GROUPS, HEADS_PER_GROUP, BAND, 2 * BAND), F32),
        compiler_params=_cparams(("arbitrary",)),
        name="bias_prompt",
    )(jnp.asarray(bkt), rel_bias)


def _bias_sample_kernel(bkt_ref, tab_ref, o_ref, *, n_new):
    g = pl.program_id(0)
    H = HEADS_PER_GROUP
    bkt = bkt_ref[...]
    for h in range(H):
        c = g * H + h
        row = jnp.zeros(bkt.shape, F32)
        for b in range(N_BUCKETS):
            row = jnp.where(bkt == b, tab_ref[b, c], row)
        own = jnp.zeros(bkt.shape, F32) + tab_ref[0, c]
        for t in range(n_new):
            o_ref[0, t * H + h: t * H + h + 1, :] = row
            o_ref[1, t * H + h: t * H + h + 1, :] = own


def _bias_sample(rel_bias, n_new):
    rows = n_new * HEADS_PER_GROUP
    bkt = np.stack([_rel_bucket_np((BAND - np.arange(BAND)) * dil) for _, dil in DIL_GROUPS])[:, None, :]
    return pl.pallas_call(
        functools.partial(_bias_sample_kernel, n_new=n_new),
        grid=(N_GROUPS,),
        in_specs=[pl.BlockSpec((None, 1, BAND), lambda g: (g, 0, 0)),
                  pl.BlockSpec(memory_space=pltpu.SMEM)],
        out_specs=pl.BlockSpec((None, 2, rows, BAND), lambda g: (g, 0, 0, 0)),
        out_shape=jax.ShapeDtypeStruct((N_GROUPS, 2, rows, BAND), F32),
        compiler_params=_cparams(("arbitrary",)),
        name="bias_sample",
    )(jnp.asarray(bkt), rel_bias)


def _prompt_attn_kernel(*refs):
    ins, (o_ref, acc_ref, ml_ref) = refs[:-3], refs[-3:]
    first_tile = pl.program_id(1) == 0
    row = lax.broadcasted_iota(jnp.int32, (BAND, BAND), 0)
    col = lax.broadcasted_iota(jnp.int32, (BAND, BAND), 1)
    keep_cur = col <= row
    keep_prev = col >= row
    keep_prev_first = keep_prev & jnp.logical_not(first_tile)

    for g, (_, dil) in enumerate(DIL_GROUPS):
        q_ref, kc_ref, vc_ref, kp_ref, vp_ref, bias_ref = ins[6 * g: 6 * g + 6]
        bias_prev = bias_ref[:, 0:BAND]
        bias_cur = bias_ref[:, BAND:2 * BAND]
        for n in range(ATTN_TILE // (BAND * dil)):
            for r in range(dil):
                cur = pl.ds(n * BAND * dil + r, BAND, stride=dil)
                q = q_ref[cur, :].astype(BF16)
                kc = kc_ref[cur, :]
                vc = vc_ref[cur, :]
                if n == 0:
                    prev = pl.ds(r, BAND, stride=dil)
                    kp, vp, mask_prev = kp_ref[prev, :], vp_ref[prev, :], keep_prev_first
                else:
                    prev = pl.ds((n - 1) * BAND * dil + r, BAND, stride=dil)
                    kp, vp, mask_prev = kc_ref[prev, :], vc_ref[prev, :], keep_prev
                s_c = jnp.where(keep_cur, _dot_nt(q, kc) * SCALE + bias_cur, NEG)
                s_p = jnp.where(mask_prev, _dot_nt(q, kp) * SCALE + bias_prev, NEG)
                m = jnp.maximum(jnp.max(s_c, axis=-1, keepdims=True), jnp.max(s_p, axis=-1, keepdims=True))
                p_c = jnp.exp(s_c - m)
                p_p = jnp.exp(s_p - m)
                l = jnp.sum(p_c, axis=-1, keepdims=True) + jnp.sum(p_p, axis=-1, keepdims=True)
                acc_ref[g, cur, :] = _dot(p_c, vc) + _dot(p_p, vp)
                ml_ref[g, cur, :] = m
                ml_ref[N_GROUPS + g, cur, :] = l

    chunk = 256
    for c in range(ATTN_TILE // chunk):
        rows = slice(c * chunk, (c + 1) * chunk)
        ms = [ml_ref[g, rows, :] for g in range(N_GROUPS)]
        m = functools.reduce(jnp.maximum, ms)
        num = jnp.zeros((chunk, HEAD_DIM), F32)
        den = jnp.zeros((chunk, 1), F32)
        for g in range(N_GROUPS):
            w = jnp.exp(ms[g] - m)
            num = num + w * acc_ref[g, rows, :]
            den = den + w * ml_ref[N_GROUPS + g, rows, :]
        o_ref[rows, :] = num / den


def _prompt_attn(proj, bias_p, batch, seq):
    H = HEADS_PER_GROUP
    assert seq % ATTN_TILE == 0
    ntile = seq // ATTN_TILE
    args, specs = [], []
    for g, (win, dil) in enumerate(DIL_GROUPS):
        assert win // dil == BAND
        span = BAND * dil
        per = ATTN_TILE // span
        qc, kc, vc = g * H, (N_GROUPS + g) * H, (2 * N_GROUPS + g) * H
        cur = lambda c: (lambda b, i, h: (b * ntile + i, c + h))
        prev = lambda c, per=per: (lambda b, i, h: (jnp.maximum((b * ntile + i) * per - 1, 0), c + h))
        specs += [pl.BlockSpec((ATTN_TILE, HEAD_DIM), cur(qc)),
                  pl.BlockSpec((ATTN_TILE, HEAD_DIM), cur(kc)),
                  pl.BlockSpec((ATTN_TILE, HEAD_DIM), cur(vc)),
                  pl.BlockSpec((span, HEAD_DIM), prev(kc)),
                  pl.BlockSpec((span, HEAD_DIM), prev(vc)),
                  pl.BlockSpec((None, None, BAND, 2 * BAND), lambda b, i, h, g=g: (g, h, 0, 0))]
        args += [proj] * 5 + [bias_p]
    return pl.pallas_call(
        _prompt_attn_kernel,
        grid=(batch, ntile, H),
        in_specs=specs,
        out_specs=pl.BlockSpec((ATTN_TILE, HEAD_DIM), lambda b, i, h: (b * ntile + i, h)),
        out_shape=jax.ShapeDtypeStruct((batch * seq, WIDTH_A_OUT), F32),
        scratch_shapes=[pltpu.VMEM((N_GROUPS, ATTN_TILE, HEAD_DIM), F32),
                        pltpu.VMEM((2 * N_GROUPS, ATTN_TILE, 1), F32)],
        compiler_params=_cparams(("parallel", "arbitrary", "arbitrary")),
        name="prompt_attn",
    )(*args)


def _mem_attn_kernel(q_ref, k_ref, v_ref, o_ref):
    for h in range(MEM_HEADS):
        hs = slice(h * HEAD_DIM, (h + 1) * HEAD_DIM)
        s = _dot_nt(q_ref[:, hs], k_ref[:, hs]) * SCALE
        m = jnp.max(s, axis=-1, keepdims=True)
        p = jnp.exp(s - m)
        l = jnp.sum(p, axis=-1, keepdims=True)
        o_ref[:, hs] = _dot(p, v_ref[:, hs]) / l


def _mem_attn(proj, qcol, mk, mv, layer, batch, seq, tm):
    steps = seq // tm
    return pl.pallas_call(
        _mem_attn_kernel,
        grid=(batch, steps),
        in_specs=[pl.BlockSpec((tm, WIDTH_MEM), lambda b, i: (b * steps + i, qcol)),
                  pl.BlockSpec((None, N_MEM, WIDTH_MEM), lambda b, i: (layer, b, 0)),
                  pl.BlockSpec((None, N_MEM, WIDTH_MEM), lambda b, i: (layer, b, 0))],
        out_specs=pl.BlockSpec((tm, WIDTH_MEM), lambda b, i: (b * steps + i, 0)),
        out_shape=jax.ShapeDtypeStruct((batch * seq, WIDTH_MEM), F32),
        compiler_params=_cparams(("parallel", "parallel")),
        name="mem_attn",
    )(proj, mk, mv)


def _out_proj_kernel(a1_ref, a2_ref, z1_ref, z2_ref, x_ref, w1_ref, w2_ref, o_ref):
    g1 = (a1_ref[...] * _silu(z1_ref[...])).astype(BF16)
    g2 = (a2_ref[...] * _silu(z2_ref[...])).astype(BF16)
    y = _mm(g1, w1_ref[...])
    y = y + _mm(g2, w2_ref[...])
    o_ref[...] = x_ref[...] + y


def _out_proj(a1, a2, proj, z1col, z2col, x, w1, w2, tm):
    m, d = x.shape
    k1, k2 = a1.shape[1], a2.shape[1]
    return pl.pallas_call(
        _out_proj_kernel,
        grid=(m // tm,),
        in_specs=[pl.BlockSpec((tm, k1), lambda i: (i, 0)),
                  pl.BlockSpec((tm, k2), lambda i: (i, 0)),
                  pl.BlockSpec((tm, k1), lambda i: (i, z1col)),
                  pl.BlockSpec((tm, k2), lambda i: (i, z2col)),
                  pl.BlockSpec((tm, d), lambda i: (i, 0)),
                  pl.BlockSpec((k1, d), lambda i: (0, 0)),
                  pl.BlockSpec((k2, d), lambda i: (0, 0))],
        out_specs=pl.BlockSpec((tm, d), lambda i: (i, 0)),
        out_shape=jax.ShapeDtypeStruct((m, d), F32),
        input_output_aliases={4: 0},
        compiler_params=_cparams(("parallel",)),
        name="out_proj",
    )(a1, a2, proj, proj, x, w1, w2)


def _shift_copy(src_ref, dst_ref, drop, chunk=256):
    total = src_ref.shape[0] - drop
    steps = -(-total // chunk)

    def body(i, carry):
        start = pl.multiple_of(jnp.minimum(i * chunk, total - chunk), 8)
        dst_ref[pl.ds(start, chunk), :] = src_ref[pl.ds(start + drop, chunk), :]
        return carry

    lax.fori_loop(0, steps, body, 0)


def _sample_win_body(new_ref, cin, bias_ref, cout, o_ref, kcat_ref, vcat_ref, n_new):
    H = HEADS_PER_GROUP
    R = n_new * H
    NPART = 4
    qb, kb, vb = 0, N_GROUPS * H, 2 * N_GROUPS * H
    rows_of = lambda base, g: jnp.concatenate(
        [new_ref[t, base + g * H: base + (g + 1) * H, :] for t in range(n_new)], axis=0)

    for g in range(N_GROUPS):
        for (src, dst, base) in ((cin[g][0], cout[g][0], kb), (cin[g][1], cout[g][1], vb)):
            rows = src.shape[0]
            _shift_copy(src, dst, R)
            dst[rows - R: rows, :] = rows_of(base, g)

    rows0 = cin[0][0].shape[0]
    kcat_ref[0:rows0, :] = cin[0][0][...]
    vcat_ref[0:rows0, :] = cin[0][1][...]
    kcat_ref[rows0:rows0 + R, :] = rows_of(kb, 0)
    vcat_ref[rows0:rows0 + R, :] = rows_of(vb, 0)

    lane = lax.broadcasted_iota(jnp.int32, (R, BAND), 1)
    parts = []
    for g, (win, dil) in enumerate(DIL_GROUPS):
        q = rows_of(qb, g)
        k_own = rows_of(kb, g)
        v_own = rows_of(vb, g)
        ksrc, vsrc = (kcat_ref, vcat_ref) if g == 0 else cin[g]
        pitch = dil * H
        sc = [jnp.zeros((R, BAND), F32) for _ in range(NPART)]
        for s in range(BAND):
            red = jnp.sum(ksrc[s * pitch: s * pitch + R, :] * q, axis=-1, keepdims=True)
            sc[s % NPART] = jnp.where(lane == s, red, sc[s % NPART])
        scores = ((sc[0] + sc[1]) + (sc[2] + sc[3])) * SCALE + bias_ref[g, 0]
        s_own = jnp.sum(k_own * q, axis=-1, keepdims=True) * SCALE + bias_ref[g, 1][:, 0:1]
        m = jnp.maximum(jnp.max(scores, axis=-1, keepdims=True), s_own)
        p = jnp.exp(scores - m)
        p_own = jnp.exp(s_own - m)
        l = jnp.sum(p, axis=-1, keepdims=True) + p_own
        acc = [jnp.zeros((R, HEAD_DIM), F32) for _ in range(NPART)]
        for s in range(BAND):
            acc[s % NPART] = acc[s % NPART] + p[:, s:s + 1] * vsrc[s * pitch: s * pitch + R, :]
        parts.append((((acc[0] + acc[1]) + (acc[2] + acc[3])) + p_own * v_own, m, l))

    m = functools.reduce(jnp.maximum, [pt[1] for pt in parts])
    num = jnp.zeros((R, HEAD_DIM), F32)
    den = jnp.zeros((R, 1), F32)
    for acc, mg, lg in parts:
        w = jnp.exp(mg - m)
        num = num + w * acc
        den = den + w * lg
    o = num / den
    for t in range(n_new):
        o_ref[t] = o[t * H:(t + 1) * H, :]


def _sample_mem_body(q_ref, qblk, k_ref, v_ref, o_ref, n_new):
    pad = jnp.zeros((8 - n_new, HEAD_DIM), F32)
    scores = []
    for h in range(MEM_HEADS):
        q = jnp.concatenate([q_ref[t, qblk + h: qblk + h + 1, :] for t in range(n_new)] + [pad], axis=0)
        scores.append(_dot_nt(q, k_ref[pl.ds(h, N_MEM, stride=MEM_HEADS), :]) * SCALE)
    probs = []
    for s in scores:
        p = jnp.exp(s - jnp.max(s, axis=-1, keepdims=True))
        probs.append((p, jnp.sum(p, axis=-1, keepdims=True)))
    for h, (p, l) in enumerate(probs):
        o = _dot(p, v_ref[pl.ds(h, N_MEM, stride=MEM_HEADS), :]) / l
        for t in range(n_new):
            o_ref[t, h:h + 1, :] = o[t:t + 1, :]


def _gate_terms(ba, alog_row, dtb_row):
    beta = _sigmoid(ba)
    g = -jnp.exp(alog_row) * _softplus(ba + dtb_row)
    return beta, g


def _delta_prompt_body(x_ref, halo_ref, ba_ref, cw_ref, alog_ref, dtb_ref, onw_ref,
                       o_ref, s_out_ref, xs_ref, s_ref):
    C = BLOCK_ROWS
    NH = N_HEADS_B
    it = pl.program_id(1)

    @pl.when(it == 0)
    def _():
        s_ref[...] = jnp.zeros(s_ref.shape, F32)
        xs_ref[0:8, :] = jnp.zeros((8, xs_ref.shape[1]), F32)

    @pl.when(it > 0)
    def _():
        xs_ref[0:8, :] = halo_ref[...]

    xs_ref[8:8 + C, :] = x_ref[...]

    def conv_block(c0):
        cs = slice(c0, c0 + HEAD_DIM)
        acc = xs_ref[pl.ds(8 - (CONV_W - 1), C), cs] * cw_ref[0:1, cs]
        for wi in range(1, CONV_W):
            acc = acc + xs_ref[pl.ds(8 - (CONV_W - 1) + wi, C), cs] * cw_ref[wi:wi + 1, cs]
        return _silu(acc)

    beta, g = _gate_terms(ba_ref[...], alog_ref[...], dtb_ref[...])
    row = lax.broadcasted_iota(jnp.int32, (C, C), 0)
    col = lax.broadcasted_iota(jnp.int32, (C, C), 1)
    tri = row >= col
    strict = row > col
    ones_tri = jnp.where(tri, 1.0, 0.0).astype(BF16)
    gh, gm, gl = _split3(g)
    gc = _mm(ones_tri, gh) + (_mm(ones_tri, gm) + _mm(ones_tri, gl))
    gc_t = gc.T
    eye = jnp.where(row == col, 1.0, 0.0)

    q, k, v, kb, egc, decay, b_col, gc_col = [], [], [], [], [], [], [], []
    m_hi, m_lo, tinv = [], [], []
    for h in range(NH):
        q.append(_l2norm(conv_block(h * HEAD_DIM)) * SCALE)
        k.append(_l2norm(conv_block(WIDTH_B + h * HEAD_DIM)))
        v.append(conv_block(2 * WIDTH_B + h * HEAD_DIM))
        b_col.append(beta[:, h:h + 1])
        gc_col.append(gc[:, 8 + h: 9 + h])
        gc_row = gc_t[8 + h: 9 + h, :]
        decay.append(jnp.where(tri, jnp.exp(jnp.where(tri, gc_col[h] - gc_row, 0.0)), 0.0))
        kb.append(k[h] * b_col[h])
        egc.append(jnp.exp(gc_col[h]))
        lmat = jnp.where(strict, _dot_nt(kb[h], k[h]) * decay[h], 0.0)
        hi, lo = _split2(-lmat)
        m_hi.append(hi)
        m_lo.append(lo)
        tinv.append(eye - lmat)

    for _ in range(int(math.log2(C)) - 1):
        for h in range(NH):
            sq = _mm(m_hi[h], m_hi[h]) + (_mm(m_hi[h], m_lo[h]) + _mm(m_lo[h], m_hi[h]))
            m_hi[h], m_lo[h] = _split2(sq)
        for h in range(NH):
            t_hi, t_lo = _split2(tinv[h])
            tinv[h] = tinv[h] + (_mm(t_hi, m_hi[h]) + (_mm(t_hi, m_lo[h]) + _mm(t_lo, m_hi[h])))

    uw = [_dot(tinv[h], jnp.concatenate([v[h] * b_col[h], kb[h] * egc[h]], axis=1)) for h in range(NH)]
    aqk = [_dot_nt(q[h], k[h]) * decay[h] for h in range(NH)]
    for h in range(NH):
        hs = slice(h * HEAD_DIM, (h + 1) * HEAD_DIM)
        g_last = gc_col[h][C - 1:C, :]
        kdec = k[h] * jnp.exp(g_last - gc_col[h])
        s_prev = s_ref[h]
        rs = _dot(jnp.concatenate([uw[h][:, HEAD_DIM:], q[h] * egc[h]], axis=0), s_prev)
        v_new = uw[h][:, :HEAD_DIM] - rs[0:C]
        o = rs[C:2 * C] + _dot(aqk[h], v_new)
        s_ref[h] = s_prev * jnp.exp(g_last) + _dot(kdec.T, v_new)
        ms = jnp.mean(o * o, axis=-1, keepdims=True)
        o_ref[:, hs] = o * lax.rsqrt(ms + EPS) * onw_ref[...]

    @pl.when(it == pl.num_programs(1) - 1)
    def _():
        s_out_ref[...] = s_ref[...]


N_DP_IN = 7


def _prompt_delta_sample_win_kernel(*refs, n_new, n_alias):
    dp_in = refs[:N_DP_IN]
    new_ref = refs[N_DP_IN]
    c = refs[N_DP_IN + 1: N_DP_IN + 7]
    bias_ref, memk_ref, memv_ref = refs[N_DP_IN + 7: N_DP_IN + 10]
    outs = refs[N_DP_IN + 10 + n_alias:]
    o_ref, s_out_ref = outs[0:2]
    co = outs[2:8]
    os_ref, om_ref = outs[8:10]
    xs_ref, s_ref, kcat_ref, vcat_ref = outs[10:14]
    _delta_prompt_body(*dp_in, o_ref, s_out_ref, xs_ref, s_ref)
    _sample_win_body(new_ref, ((c[0], c[1]), (c[2], c[3]), (c[4], c[5])), bias_ref,
                     ((co[0], co[1]), (co[2], co[3]), (co[4], co[5])), os_ref, kcat_ref, vcat_ref, n_new)
    _sample_mem_body(new_ref, 4 * 9, memk_ref, memv_ref, om_ref, n_new)


def _prompt_delta_sample_win(proj_p, conv_w, alog_row, dtb_row, onw, bp, seq,
                             proj_s, caches, win_layer, prev_out, bias_s, mem_k, mem_v, mem_layer, bs, n_new):
    C = BLOCK_ROWS
    H = HEADS_PER_GROUP
    steps = seq // C
    assert bs == bp * steps and n_new * H % 8 == 0
    for g, (win, dil) in enumerate(DIL_GROUPS):
        assert caches[2 * g].shape[2] == win * H and win // dil == BAND
    wq = 3 * WIDTH_B
    sb = lambda b, i: b * steps + i
    const2 = lambda shape: pl.BlockSpec(shape, lambda b, i: (0, 0))
    new = proj_s.reshape(bs, n_new, IN_A // 128, 128)
    cspecs = [pl.BlockSpec((None, None) + c.shape[2:], lambda b, i: (win_layer, sb(b, i), 0, 0)) for c in caches]
    mspec = pl.BlockSpec((None, None, N_MEM * MEM_HEADS, HEAD_DIM), lambda b, i: (mem_layer, sb(b, i), 0, 0))
    rows0 = caches[0].shape[2]
    alias_args = list(prev_out) if prev_out is not None else []
    n_alias = len(alias_args)
    n_in = N_DP_IN + 10
    small = lambda w: pl.BlockSpec((None, n_new, H, w), lambda b, i: (sb(b, i), 0, 0, 0))
    outs = pl.pallas_call(
        functools.partial(_prompt_delta_sample_win_kernel, n_new=n_new, n_alias=n_alias),
        grid=(bp, steps),
        in_specs=[pl.BlockSpec((C, wq), lambda b, i: (sb(b, i), 0)),
                  pl.BlockSpec((8, wq), lambda b, i: (jnp.maximum(sb(b, i) * (C // 8) - 1, 0), 0)),
                  pl.BlockSpec((C, 128), lambda b, i: (sb(b, i), BA_COL_BLOCK)),
                  const2((CONV_W, wq)), const2((1, 128)), const2((1, 128)), const2((1, HEAD_DIM)),
                  pl.BlockSpec((None, n_new, IN_A // 128, 128), lambda b, i: (sb(b, i), 0, 0, 0))]
                 + cspecs
                 + [pl.BlockSpec(bias_s.shape, lambda b, i: (0, 0, 0, 0)), mspec, mspec]
                 + [pl.BlockSpec(memory_space=pl.ANY)] * n_alias,
        out_specs=[pl.BlockSpec((C, WIDTH_B), lambda b, i: (sb(b, i), 0)),
                   pl.BlockSpec((None, N_HEADS_B, HEAD_DIM, HEAD_DIM), lambda b, i: (b, 0, 0, 0))]
                  + cspecs + [small(HEAD_DIM), small(HEAD_DIM)],
        out_shape=[jax.ShapeDtypeStruct((bp * seq, WIDTH_B), F32),
                   jax.ShapeDtypeStruct((bp, N_HEADS_B, HEAD_DIM, HEAD_DIM), F32)]
                  + [jax.ShapeDtypeStruct(c.shape, F32) for c in caches]
                  + [jax.ShapeDtypeStruct((bs, n_new, H, HEAD_DIM), F32),
                     jax.ShapeDtypeStruct((bs, n_new, MEM_HEADS, HEAD_DIM), F32)],
        scratch_shapes=[pltpu.VMEM((8 + C, wq), F32),
                        pltpu.VMEM((N_HEADS_B, HEAD_DIM, HEAD_DIM), F32),
                        pltpu.VMEM((rows0 + n_new * H, 128), F32),
                        pltpu.VMEM((rows0 + n_new * H, 128), F32)],
        input_output_aliases={n_in + k: 2 + k for k in range(n_alias)},
        compiler_params=pltpu.CompilerParams(dimension_semantics=("parallel", "arbitrary"),
                                             vmem_limit_bytes=FUSED_VMEM_LIMIT),
        name="prompt_delta_sample_win",
    )(proj_p, proj_p, proj_p, conv_w, alog_row, dtb_row, onw, new, *caches, bias_s, mem_k, mem_v, *alias_args)
    o_p, s_p = outs[0], outs[1]
    return (o_p, s_p, outs[2:8], outs[8].reshape(bs * n_new, WIDTH_A_OUT), outs[9].reshape(bs * n_new, WIDTH_MEM))


def _delta_sample_kernel(*refs, n_new, n_alias):
    NH = N_HEADS_B
    x_ref, cs_ref, s_in_ref, cw_ref, alog_ref, dtb_ref, onw_ref, memk_ref, memv_ref = refs[:9]
    o_ref, s_out_ref, om_ref, st_ref, rs_ref, kd_ref, vn_ref = refs[9 + n_alias:]
    qkv_rows = 3 * NH

    _sample_mem_body(x_ref, 4 * 9, memk_ref, memv_ref, om_ref, n_new)

    xp = [cs_ref[i] for i in range(CONV_W - 1)] + [x_ref[t, 0:qkv_rows, :] for t in range(n_new)]
    ba = jnp.concatenate([x_ref[t, BA_COL_BLOCK:BA_COL_BLOCK + 1, :] for t in range(n_new)]
                         + [jnp.zeros((HEAD_DIM - n_new, 128), F32)], axis=0)
    ba_t = ba.T
    alog_col = alog_ref[:, 0:1]
    dtb_col = dtb_ref[:, 0:1]

    q, k, v, beta, gc = [], [], [], [], []
    run = jnp.zeros((NH, 1), F32)
    for t in range(n_new):
        acc = xp[t] * cw_ref[0]
        for wi in range(1, CONV_W):
            acc = acc + xp[t + wi] * cw_ref[wi]
        conv = _silu(acc)
        q.append(_l2norm(conv[0:NH]) * SCALE)
        k.append(_l2norm(conv[NH:2 * NH]))
        v.append(conv[2 * NH:3 * NH])
        beta.append(_sigmoid(ba_t[0:NH, t:t + 1]))
        run = run - jnp.exp(alog_col) * _softplus(ba_t[NH:2 * NH, t:t + 1] + dtb_col)
        gc.append(run)
    egc = [jnp.exp(x) for x in gc]
    dec = {(i, j): jnp.exp(gc[i] - gc[j]) for i in range(n_new) for j in range(i)}
    lsum = lambda a, b: jnp.sum(a * b, axis=-1, keepdims=True)

    u, w = [], []
    for i in range(n_new):
        ui = v[i] * beta[i]
        wi_ = k[i] * (beta[i] * egc[i])
        for j in range(i):
            lij = beta[i] * lsum(k[i], k[j]) * dec[i, j]
            ui = ui - lij * u[j]
            wi_ = wi_ - lij * w[j]
        u.append(ui)
        w.append(wi_)

    for i in range(n_new):
        st_ref[i * NH:(i + 1) * NH, :] = w[i]
        st_ref[(n_new + i) * NH:(n_new + i + 1) * NH, :] = q[i] * egc[i]
    for h in range(NH):
        rs_ref[h * 8:(h + 1) * 8, :] = _dot(st_ref[pl.ds(h, 2 * n_new, stride=NH), :], s_in_ref[h])

    vn, g_last = [], gc[n_new - 1]
    for i in range(n_new):
        vn.append(u[i] - rs_ref[pl.ds(i, NH, stride=8), :])
    for i in range(n_new):
        o = rs_ref[pl.ds(n_new + i, NH, stride=8), :] + lsum(q[i], k[i]) * vn[i]
        for j in range(i):
            o = o + (lsum(q[i], k[j]) * dec[i, j]) * vn[j]
        ms = jnp.mean(o * o, axis=-1, keepdims=True)
        o_ref[i] = o * lax.rsqrt(ms + EPS) * onw_ref[...]

    kd_ref[...] = jnp.zeros(kd_ref.shape, F32)
    vn_ref[...] = jnp.zeros(vn_ref.shape, F32)
    for i in range(n_new):
        kd_ref[i * NH:(i + 1) * NH, :] = k[i] * jnp.exp(g_last - gc[i])
        vn_ref[i * NH:(i + 1) * NH, :] = vn[i]
    eg_last = jnp.exp(g_last)
    zpad = jnp.zeros((HEAD_DIM - 8, HEAD_DIM), F32)
    for h in range(NH):
        kd = jnp.concatenate([kd_ref[pl.ds(h, 8, stride=NH), :], zpad], axis=0)
        vh = jnp.concatenate([vn_ref[pl.ds(h, 8, stride=NH), :], zpad], axis=0)
        s_out_ref[h] = s_in_ref[h] * eg_last[h:h + 1, :] + _dot(kd.T, vh)


def _delta_sample(proj_s, conv_state, delta_state, layer, prev_out, conv_w, a_log, dt_bias, onw,
                  mem_k, mem_v, mem_layer, batch, n_new):
    assert n_new == 4 and N_HEADS_B == 8
    ncol = IN_B_PAD // 128
    new = proj_s.reshape(batch, n_new, ncol, 128)
    cw = conv_w.reshape(CONV_W, 3 * N_HEADS_B, HEAD_DIM)
    alog = jnp.broadcast_to(a_log[:, None], (N_HEADS_B, 128))
    dtb = jnp.broadcast_to(dt_bias[:, None], (N_HEADS_B, 128))
    sspec = pl.BlockSpec((None, None, N_HEADS_B, HEAD_DIM, HEAD_DIM), lambda b: (layer, b, 0, 0, 0))
    mspec = pl.BlockSpec((None, None, N_MEM * MEM_HEADS, HEAD_DIM), lambda b: (mem_layer, b, 0, 0))
    alias_args = [prev_out] if prev_out is not None else []
    n_alias = len(alias_args)
    n_in = 9
    o, s, om = pl.pallas_call(
        functools.partial(_delta_sample_kernel, n_new=n_new, n_alias=n_alias),
        grid=(batch,),
        in_specs=[pl.BlockSpec((None, n_new, ncol, 128), lambda b: (b, 0, 0, 0)),
                  pl.BlockSpec((None, None, CONV_W - 1, 3 * N_HEADS_B, HEAD_DIM), lambda b: (layer, b, 0, 0, 0)),
                  sspec,
                  pl.BlockSpec(cw.shape, lambda b: (0, 0, 0)),
                  pl.BlockSpec((N_HEADS_B, 128), lambda b: (0, 0)),
                  pl.BlockSpec((N_HEADS_B, 128), lambda b: (0, 0)),
                  pl.BlockSpec((1, HEAD_DIM), lambda b: (0, 0)),
                  mspec, mspec]
                 + [pl.BlockSpec(memory_space=pl.ANY)] * n_alias,
        out_specs=[pl.BlockSpec((None, n_new, N_HEADS_B, HEAD_DIM), lambda b: (b, 0, 0, 0)), sspec,
                   pl.BlockSpec((None, n_new, MEM_HEADS, HEAD_DIM), lambda b: (b, 0, 0, 0))],
        out_shape=[jax.ShapeDtypeStruct((batch, n_new, N_HEADS_B, HEAD_DIM), F32),
                   jax.ShapeDtypeStruct(delta_state.shape, F32),
                   jax.ShapeDtypeStruct((batch, n_new, MEM_HEADS, HEAD_DIM), F32)],
        scratch_shapes=[pltpu.VMEM((2 * n_new * N_HEADS_B, HEAD_DIM), F32),
                        pltpu.VMEM((8 * N_HEADS_B, HEAD_DIM), F32),
                        pltpu.VMEM((8 * N_HEADS_B, HEAD_DIM), F32),
                        pltpu.VMEM((8 * N_HEADS_B, HEAD_DIM), F32)],
        input_output_aliases={n_in + k: 1 + k for k in range(n_alias)},
        compiler_params=_cparams(("parallel",)),
        name="delta_sample",
    )(new, conv_state, delta_state, cw, alog, dtb, onw, mem_k, mem_v, *alias_args)
    return o.reshape(batch * n_new, WIDTH_B), s, om.reshape(batch * n_new, WIDTH_MEM)


def _reorder_w_in_b(w):
    q0 = 3 * WIDTH_B
    b0, a0, m0, z0 = q0, q0 + N_HEADS_B, q0 + 2 * N_HEADS_B, q0 + 2 * N_HEADS_B + WIDTH_MEM
    parts = [w[:, :q0], w[:, z0:], w[:, m0:z0], w[:, b0:a0], w[:, a0:m0]]
    out = jnp.concatenate(parts, axis=1)
    return jnp.pad(out, ((0, 0), (0, IN_B_PAD - out.shape[1])))


def _lane_row(vals, offset):
    return jnp.zeros((1, 128), F32).at[0, offset:offset + vals.shape[0]].set(vals)


def kernel(x_prompt, x_sample, cache_win_k0, cache_win_v0, cache_win_k1, cache_win_v1, cache_win_k2, cache_win_v2, state_conv, state_delta, cache_mem_k, cache_mem_v, mem_prompt, norm_w, final_norm_w, rel_bias, w_in_a, w_out_a, w_in_b, conv_w, a_log, dt_bias, o_norm_w, w_out_b, w_mem_kv):
    bp, seq, d = x_prompt.shape
    bs, n_new, _ = x_sample.shape
    depth = norm_w.shape[0]
    assert depth % 2 == 0
    H = HEADS_PER_GROUP
    tm_p = 1024
    ms = bs * n_new

    xp = x_prompt.reshape(bp * seq, d)
    xs = x_sample.reshape(ms, d)

    mk_all, mv_all = _mem_kv(mem_prompt.reshape(bp * N_MEM, d), w_mem_kv.astype(BF16))
    bias_p = _bias_prompt(rel_bias)
    bias_s = _bias_sample(rel_bias, n_new)
    mem_k_s = cache_mem_k.reshape(depth, bs, N_MEM * MEM_HEADS, HEAD_DIM)
    mem_v_s = cache_mem_v.reshape(depth, bs, N_MEM * MEM_HEADS, HEAD_DIM)
    caches_in = (cache_win_k0, cache_win_v0, cache_win_k1, cache_win_v1, cache_win_k2, cache_win_v2)
    caches_flat = [c.reshape(c.shape[0], bs, c.shape[2] * H, HEAD_DIM) for c in caches_in]
    conv_state = state_conv.reshape(state_conv.shape[0], bs, CONV_W - 1, 3 * N_HEADS_B, HEAD_DIM)

    p_win = [[] for _ in range(2 * N_GROUPS)]
    p_conv, p_delta, s_conv = [], [], []
    s_win, s_delta = None, None

    for li in range(depth // 2):
        ia, ib = 2 * li, 2 * li + 1
        w_in = w_in_a[li].astype(BF16)
        w_out_even = w_out_a[li].astype(BF16)
        nw = norm_w[ia].reshape(1, d)
        proj_p = _norm_proj(xp, nw, w_in, tm_p, 2048)
        proj_s_even = _norm_proj(xs, nw, w_in, ms, 2048)
        p3 = proj_p.reshape(bp, seq, IN_A)
        for g, (win, dil) in enumerate(DIL_GROUPS):
            keep = min(win, seq)
            kcol = WIDTH_A_QKV + g * WIDTH_A_OUT
            vcol = 2 * WIDTH_A_QKV + g * WIDTH_A_OUT
            p_win[2 * g].append(p3[:, seq - keep:, kcol:kcol + WIDTH_A_OUT].reshape(bp, keep, H, HEAD_DIM))
            p_win[2 * g + 1].append(p3[:, seq - keep:, vcol:vcol + WIDTH_A_OUT].reshape(bp, keep, H, HEAD_DIM))
        o_p = _prompt_attn(proj_p, bias_p, bp, seq)
        om_p = _mem_attn(proj_p, 9, mk_all, mv_all, ia, bp, seq, 512)
        xp = _out_proj(o_p, om_p, proj_p, 10, 11, xp, w_out_even[:WIDTH_A_OUT], w_out_even[WIDTH_A_OUT:], 512)

        w_in = _reorder_w_in_b(w_in_b[li]).astype(BF16)
        w_out = w_out_b[li].astype(BF16)
        onw = o_norm_w[li].reshape(1, HEAD_DIM)
        nw = norm_w[ib].reshape(1, d)
        wq = 3 * WIDTH_B
        proj_p = _norm_proj(xp, nw, w_in, tm_p, 1792)
        o_p, s_p, s_win, o_s, om_s = _prompt_delta_sample_win(
            proj_p, conv_w[li], _lane_row(a_log[li], N_HEADS_B), _lane_row(dt_bias[li], N_HEADS_B), onw, bp, seq,
            proj_s_even, caches_flat, li, s_win, bias_s, mem_k_s, mem_v_s, ia, bs, n_new)
        p_delta.append(s_p)
        p_conv.append(proj_p.reshape(bp, seq, IN_B_PAD)[:, seq - (CONV_W - 1):, :wq])
        om_p = _mem_attn(proj_p, 9, mk_all, mv_all, ib, bp, seq, 512)
        xp = _out_proj(o_p, om_p, proj_p, 3, 8, xp, w_out[:WIDTH_B], w_out[WIDTH_B:], 512)
        xs = _out_proj(o_s, om_s, proj_s_even, 10, 11, xs, w_out_even[:WIDTH_A_OUT], w_out_even[WIDTH_A_OUT:], ms)

        proj_s = _norm_proj(xs, nw, w_in, ms, 1792)
        o_s, s_delta, om_s = _delta_sample(proj_s, conv_state, state_delta, li, s_delta, conv_w[li],
                                           a_log[li], dt_bias[li], onw, mem_k_s, mem_v_s, ib, bs, n_new)
        xcat = jnp.concatenate([state_conv[li], proj_s.reshape(bs, n_new, IN_B_PAD)[:, :, :wq]], axis=1)
        s_conv.append(xcat[:, n_new:])
        xs = _out_proj(o_s, om_s, proj_s, 3, 8, xs, w_out[:WIDTH_B], w_out[WIDTH_B:], ms)

    fnw = final_norm_w.reshape(1, d)
    y_prompt = _final_norm(xp, fnw, tm_p).reshape(bp, seq, d)
    y_sample = _final_norm(xs, fnw, ms).reshape(bs, n_new, d)
    p_mk = mk_all.reshape(depth, bp, N_MEM, MEM_HEADS, HEAD_DIM)
    p_mv = mv_all.reshape(depth, bp, N_MEM, MEM_HEADS, HEAD_DIM)
    s_win = [s_win[n].reshape(caches_in[n].shape) for n in range(2 * N_GROUPS)]
    return (y_prompt, y_sample,
            jnp.stack(p_win[0]), jnp.stack(p_win[1]), jnp.stack(p_win[2]),
            jnp.stack(p_win[3]), jnp.stack(p_win[4]), jnp.stack(p_win[5]),
            jnp.stack(p_conv), jnp.stack(p_delta), p_mk, p_mv,
            s_win[0], s_win[1], s_win[2], s_win[3], s_win[4], s_win[5],
            jnp.stack(s_conv), s_delta)
```

```python
import functools
import math

import jax
import jax.numpy as jnp
import numpy as np
from jax import lax
from jax.experimental import pallas as pl
from jax.experimental.pallas import tpu as pltpu

F32 = jnp.float32
BF16 = jnp.bfloat16

D_MODEL = 1024
HEAD_DIM = 128
DIL_GROUPS = ((128, 1), (512, 4), (2048, 16))
N_GROUPS = len(DIL_GROUPS)
HEADS_PER_GROUP = 4
WIDTH_A_QKV = N_GROUPS * HEADS_PER_GROUP * HEAD_DIM
WIDTH_A_OUT = HEADS_PER_GROUP * HEAD_DIM
N_MEM = 256
MEM_HEADS = 4
WIDTH_MEM = MEM_HEADS * HEAD_DIM
N_HEADS_B = 8
WIDTH_B = N_HEADS_B * HEAD_DIM
CONV_W = 4
N_BUCKETS = 32
MAX_EXACT = N_BUCKETS // 2
MAX_DIST = 2048
EPS = 1e-6
NEG = -1e30
SCALE = HEAD_DIM ** -0.5
BAND = 128
BLOCK_ROWS = 128
ATTN_TILE = BAND * max(d for _, d in DIL_GROUPS)
ATTN_BLOCKS_IN_FLIGHT = 4
IN_A = 3 * WIDTH_A_QKV + WIDTH_MEM + WIDTH_A_OUT + WIDTH_MEM
IN_B_PAD = 5376
BA_COL_BLOCK = (3 * WIDTH_B + WIDTH_B + WIDTH_MEM + WIDTH_MEM) // 128
VMEM_LIMIT = 48 * 1024 * 1024


def _cparams(sem):
    return pltpu.CompilerParams(dimension_semantics=sem, vmem_limit_bytes=VMEM_LIMIT)


def _sigmoid(x):
    return 1.0 / (1.0 + jnp.exp(-x))


def _silu(x):
    return x * _sigmoid(x)


def _softplus(x):
    return jnp.maximum(x, 0.0) + jnp.log(1.0 + jnp.exp(-jnp.abs(x)))


def _mm(a, b):
    return jnp.dot(a, b, preferred_element_type=F32)


def _dot(a, b):
    return _mm(a.astype(BF16), b.astype(BF16))


def _dot_nt(a, b):
    return lax.dot_general(a.astype(BF16), b.astype(BF16), (((1,), (1,)), ((), ())),
                           preferred_element_type=F32)


def _split2(a):
    hi = a.astype(BF16)
    lo = (a - hi.astype(F32)).astype(BF16)
    return hi, lo


def _split3(a):
    hi = a.astype(BF16)
    r1 = a - hi.astype(F32)
    mid = r1.astype(BF16)
    lo = (r1 - mid.astype(F32)).astype(BF16)
    return hi, mid, lo


def _l2norm(x):
    return x * lax.rsqrt(jnp.sum(x * x, axis=-1, keepdims=True) + EPS)


def _norm_proj_kernel(x_ref, nw_ref, w_ref, o_ref, h_ref):
    @pl.when(pl.program_id(1) == 0)
    def _():
        x = x_ref[...]
        ms = jnp.mean(x * x, axis=-1, keepdims=True)
        h_ref[...] = (x * lax.rsqrt(ms + EPS) * nw_ref[...]).astype(BF16)

    o_ref[...] = _mm(h_ref[...], w_ref[...])


def _norm_proj(x, nw, w, tm, tn):
    m, d = x.shape
    n = w.shape[1]
    return pl.pallas_call(
        _norm_proj_kernel,
        grid=(m // tm, n // tn),
        in_specs=[pl.BlockSpec((tm, d), lambda i, j: (i, 0)),
                  pl.BlockSpec((1, d), lambda i, j: (0, 0)),
                  pl.BlockSpec((d, tn), lambda i, j: (0, j))],
        out_specs=pl.BlockSpec((tm, tn), lambda i, j: (i, j)),
        out_shape=jax.ShapeDtypeStruct((m, n), F32),
        scratch_shapes=[pltpu.VMEM((tm, d), BF16)],
        compiler_params=_cparams(("parallel", "arbitrary")),
        name="norm_proj",
    )(x, nw, w)


def _final_norm_kernel(x_ref, nw_ref, o_ref):
    x = x_ref[...]
    ms = jnp.mean(x * x, axis=-1, keepdims=True)
    o_ref[...] = x * lax.rsqrt(ms + EPS) * nw_ref[...]


def _final_norm(x, nw, tm):
    m, d = x.shape
    return pl.pallas_call(
        _final_norm_kernel,
        grid=(m // tm,),
        in_specs=[pl.BlockSpec((tm, d), lambda i: (i, 0)),
                  pl.BlockSpec((1, d), lambda i: (0, 0))],
        out_specs=pl.BlockSpec((tm, d), lambda i: (i, 0)),
        out_shape=jax.ShapeDtypeStruct((m, d), F32),
        compiler_params=_cparams(("parallel",)),
        name="final_norm",
    )(x, nw)


def _mem_kv_kernel(x_ref, w_ref, k_ref, v_ref):
    r = _mm(x_ref[...].astype(BF16), w_ref[...])
    k_ref[...] = r[:, :WIDTH_MEM]
    v_ref[...] = r[:, WIDTH_MEM:]


def _mem_kv(mem, w):
    m, d = mem.shape
    depth = w.shape[0]
    out = jax.ShapeDtypeStruct((depth, m, WIDTH_MEM), F32)
    return pl.pallas_call(
        _mem_kv_kernel,
        grid=(depth,),
        in_specs=[pl.BlockSpec((m, d), lambda i: (0, 0)),
                  pl.BlockSpec((None, d, 2 * WIDTH_MEM), lambda i: (i, 0, 0))],
        out_specs=[pl.BlockSpec((None, m, WIDTH_MEM), lambda i: (i, 0, 0)),
                   pl.BlockSpec((None, m, WIDTH_MEM), lambda i: (i, 0, 0))],
        out_shape=[out, out],
        compiler_params=_cparams(("parallel",)),
        name="mem_kv",
    )(mem, w)


def _rel_bucket_np(dist):
    dist = np.asarray(dist, np.int64)
    df = np.maximum(dist, 1).astype(np.float32)
    large = MAX_EXACT + (np.log(df / np.float32(MAX_EXACT)) / np.float32(math.log(MAX_DIST / MAX_EXACT))
                         * np.float32(N_BUCKETS - MAX_EXACT)).astype(np.int32)
    return np.where(dist < MAX_EXACT, dist, np.minimum(large, N_BUCKETS - 1)).astype(np.int32)


def _bias_prompt_kernel(bkt_ref, tab_ref, o_ref):
    g = pl.program_id(0)
    bkt = bkt_ref[...]
    accs = [jnp.zeros(bkt.shape, F32) for _ in range(HEADS_PER_GROUP)]
    for b in range(N_BUCKETS):
        hit = bkt == b
        for h in range(HEADS_PER_GROUP):
            accs[h] = jnp.where(hit, tab_ref[b, g * HEADS_PER_GROUP + h], accs[h])
    for h in range(HEADS_PER_GROUP):
        o_ref[h] = accs[h]


def _bias_prompt(rel_bias):
    i = np.arange(BAND)[:, None]
    j = np.arange(2 * BAND)[None, :]
    dsub = np.maximum(i + BAND - j, 0)
    bkt = np.stack([_rel_bucket_np(dsub * dil) for _, dil in DIL_GROUPS])
    return pl.pallas_call(
        _bias_prompt_kernel,
        grid=(N_GROUPS,),
        in_specs=[pl.BlockSpec((None, BAND, 2 * BAND), lambda g: (g, 0, 0)),
                  pl.BlockSpec(memory_space=pltpu.SMEM)],
        out_specs=pl.BlockSpec((None, HEADS_PER_GROUP, BAND, 2 * BAND), lambda g: (g, 0, 0, 0)),
        out_shape=jax.ShapeDtypeStruct((N_GROUPS, HEADS_PER_GROUP, BAND, 2 * BAND), F32),
        compiler_params=_cparams(("arbitrary",)),
        name="bias_prompt",
    )(jnp.asarray(bkt), rel_bias)


def _bias_sample_kernel(bkt_ref, tab_ref, o_ref, *, n_new):
    g = pl.program_id(0)
    H = HEADS_PER_GROUP
    bkt = bkt_ref[...]
    for h in range(H):
        c = g * H + h
        row = jnp.zeros(bkt.shape, F32)
        for b in range(N_BUCKETS):
            row = jnp.where(bkt == b, tab_ref[b, c], row)
        own = jnp.zeros(bkt.shape, F32) + tab_ref[0, c]
        for t in range(n_new):
            o_ref[0, t * H + h: t * H + h + 1, :] = row
            o_ref[1, t * H + h: t * H + h + 1, :] = own


def _bias_sample(rel_bias, n_new):
    rows = n_new * HEADS_PER_GROUP
    bkt = np.stack([_rel_bucket_np((BAND - np.arange(BAND)) * dil) for _, dil in DIL_GROUPS])[:, None, :]
    return pl.pallas_call(
        functools.partial(_bias_sample_kernel, n_new=n_new),
        grid=(N_GROUPS,),
        in_specs=[pl.BlockSpec((None, 1, BAND), lambda g: (g, 0, 0)),
                  pl.BlockSpec(memory_space=pltpu.SMEM)],
        out_specs=pl.BlockSpec((None, 2, rows, BAND), lambda g: (g, 0, 0, 0)),
        out_shape=jax.ShapeDtypeStruct((N_GROUPS, 2, rows, BAND), F32),
        compiler_params=_cparams(("arbitrary",)),
        name="bias_sample",
    )(jnp.asarray(bkt), rel_bias)


def _prompt_attn_kernel(*refs):
    ins, (o_ref, acc_ref, ml_ref) = refs[:-3], refs[-3:]
    first_tile = pl.program_id(1) == 0
    row = lax.broadcasted_iota(jnp.int32, (BAND, BAND), 0)
    col = lax.broadcasted_iota(jnp.int32, (BAND, BAND), 1)
    keep_cur = col <= row
    keep_prev = col >= row
    keep_prev_first = keep_prev & jnp.logical_not(first_tile)

    blocks = [(g, n, r) for g, (_, dil) in enumerate(DIL_GROUPS)
              for n in range(ATTN_TILE // (BAND * dil)) for r in range(dil)]
    for first in range(0, len(blocks), ATTN_BLOCKS_IN_FLIGHT):
        staged = []
        for g, n, r in blocks[first: first + ATTN_BLOCKS_IN_FLIGHT]:
            dil = DIL_GROUPS[g][1]
            q_ref, kc_ref, vc_ref, kp_ref, vp_ref, bias_ref = ins[6 * g: 6 * g + 6]
            cur = pl.ds(n * BAND * dil + r, BAND, stride=dil)
            q = q_ref[cur, :].astype(BF16)
            kc = kc_ref[cur, :]
            if n == 0:
                prev_src, prev, mask_prev = (kp_ref, vp_ref), pl.ds(r, BAND, stride=dil), keep_prev_first
            else:
                prev_src, prev, mask_prev = (kc_ref, vc_ref), pl.ds((n - 1) * BAND * dil + r, BAND, stride=dil), keep_prev
            s_c = jnp.where(keep_cur, _dot_nt(q, kc) * SCALE + bias_ref[:, BAND:2 * BAND], NEG)
            s_p = jnp.where(mask_prev, _dot_nt(q, prev_src[0][prev, :]) * SCALE + bias_ref[:, 0:BAND], NEG)
            staged.append((g, cur, vc_ref, prev_src[1], prev, s_c, s_p))
        probs = []
        for g, cur, vc_ref, vp_src, prev, s_c, s_p in staged:
            m = jnp.maximum(jnp.max(s_c, axis=-1, keepdims=True), jnp.max(s_p, axis=-1, keepdims=True))
            p_c = jnp.exp(s_c - m)
            p_p = jnp.exp(s_p - m)
            l = jnp.sum(p_c, axis=-1, keepdims=True) + jnp.sum(p_p, axis=-1, keepdims=True)
            probs.append((p_c, p_p, m, l))
        for (g, cur, vc_ref, vp_src, prev, _, _), (p_c, p_p, m, l) in zip(staged, probs):
            acc_ref[g, cur, :] = _dot(p_c, vc_ref[cur, :]) + _dot(p_p, vp_src[prev, :])
            ml_ref[g, cur, :] = m
            ml_ref[N_GROUPS + g, cur, :] = l

    chunk = 256
    for c in range(ATTN_TILE // chunk):
        rows = slice(c * chunk, (c + 1) * chunk)
        ms = [ml_ref[g, rows, :] for g in range(N_GROUPS)]
        m = functools.reduce(jnp.maximum, ms)
        num = jnp.zeros((chunk, HEAD_DIM), F32)
        den = jnp.zeros((chunk, 1), F32)
        for g in range(N_GROUPS):
            w = jnp.exp(ms[g] - m)
            num = num + w * acc_ref[g, rows, :]
            den = den + w * ml_ref[N_GROUPS + g, rows, :]
        o_ref[rows, :] = num / den


def _prompt_attn(proj, bias_p, batch, seq):
    H = HEADS_PER_GROUP
    assert seq % ATTN_TILE == 0
    ntile = seq // ATTN_TILE
    args, specs = [], []
    for g, (win, dil) in enumerate(DIL_GROUPS):
        assert win // dil == BAND
        span = BAND * dil
        per = ATTN_TILE // span
        qc, kc, vc = g * H, (N_GROUPS + g) * H, (2 * N_GROUPS + g) * H
        cur = lambda c: (lambda b, i, h: (b * ntile + i, c + h))
        prev = lambda c, per=per: (lambda b, i, h: (jnp.maximum((b * ntile + i) * per - 1, 0), c + h))
        specs += [pl.BlockSpec((ATTN_TILE, HEAD_DIM), cur(qc)),
                  pl.BlockSpec((ATTN_TILE, HEAD_DIM), cur(kc)),
                  pl.BlockSpec((ATTN_TILE, HEAD_DIM), cur(vc)),
                  pl.BlockSpec((span, HEAD_DIM), prev(kc)),
                  pl.BlockSpec((span, HEAD_DIM), prev(vc)),
                  pl.BlockSpec((None, None, BAND, 2 * BAND), lambda b, i, h, g=g: (g, h, 0, 0))]
        args += [proj] * 5 + [bias_p]
    return pl.pallas_call(
        _prompt_attn_kernel,
        grid=(batch, ntile, H),
        in_specs=specs,
        out_specs=pl.BlockSpec((ATTN_TILE, HEAD_DIM), lambda b, i, h: (b * ntile + i, h)),
        out_shape=jax.ShapeDtypeStruct((batch * seq, WIDTH_A_OUT), F32),
        scratch_shapes=[pltpu.VMEM((N_GROUPS, ATTN_TILE, HEAD_DIM), F32),
                        pltpu.VMEM((2 * N_GROUPS, ATTN_TILE, 1), F32)],
        compiler_params=_cparams(("parallel", "arbitrary", "arbitrary")),
        name="prompt_attn",
    )(*args)


def _mem_attn_kernel(q_ref, k_ref, v_ref, o_ref):
    for h in range(MEM_HEADS):
        hs = slice(h * HEAD_DIM, (h + 1) * HEAD_DIM)
        s = _dot_nt(q_ref[:, hs], k_ref[:, hs]) * SCALE
        m = jnp.max(s, axis=-1, keepdims=True)
        p = jnp.exp(s - m)
        l = jnp.sum(p, axis=-1, keepdims=True)
        o_ref[:, hs] = _dot(p, v_ref[:, hs]) / l


def _mem_attn(proj, qcol, mk, mv, layer, batch, seq, tm):
    steps = seq // tm
    return pl.pallas_call(
        _mem_attn_kernel,
        grid=(batch, steps),
        in_specs=[pl.BlockSpec((tm, WIDTH_MEM), lambda b, i: (b * steps + i, qcol)),
                  pl.BlockSpec((None, N_MEM, WIDTH_MEM), lambda b, i: (layer, b, 0)),
                  pl.BlockSpec((None, N_MEM, WIDTH_MEM), lambda b, i: (layer, b, 0))],
        out_specs=pl.BlockSpec((tm, WIDTH_MEM), lambda b, i: (b * steps + i, 0)),
        out_shape=jax.ShapeDtypeStruct((batch * seq, WIDTH_MEM), F32),
        compiler_params=_cparams(("parallel", "parallel")),
        name="mem_attn",
    )(proj, mk, mv)


def _out_proj_kernel(a1_ref, a2_ref, z1_ref, z2_ref, x_ref, w1_ref, w2_ref, o_ref):
    g1 = (a1_ref[...] * _silu(z1_ref[...])).astype(BF16)
    g2 = (a2_ref[...] * _silu(z2_ref[...])).astype(BF16)
    y = _mm(g1, w1_ref[...])
    y = y + _mm(g2, w2_ref[...])
    o_ref[...] = x_ref[...] + y


def _out_proj(a1, a2, proj, z1col, z2col, x, w1, w2, tm, in_place):
    m, d = x.shape
    k1, k2 = a1.shape[1], a2.shape[1]
    return pl.pallas_call(
        _out_proj_kernel,
        grid=(m // tm,),
        in_specs=[pl.BlockSpec((tm, k1), lambda i: (i, 0)),
                  pl.BlockSpec((tm, k2), lambda i: (i, 0)),
                  pl.BlockSpec((tm, k1), lambda i: (i, z1col)),
                  pl.BlockSpec((tm, k2), lambda i: (i, z2col)),
                  pl.BlockSpec((tm, d), lambda i: (i, 0)),
                  pl.BlockSpec((k1, d), lambda i: (0, 0)),
                  pl.BlockSpec((k2, d), lambda i: (0, 0))],
        out_specs=pl.BlockSpec((tm, d), lambda i: (i, 0)),
        out_shape=jax.ShapeDtypeStruct((m, d), F32),
        input_output_aliases={4: 0} if in_place else {},
        compiler_params=_cparams(("parallel",)),
        name="out_proj",
    )(a1, a2, proj, proj, x, w1, w2)


def _new_rows(new_ref, base, g, n_new):
    H = HEADS_PER_GROUP
    return jnp.concatenate([new_ref[t, base + g * H: base + (g + 1) * H, :] for t in range(n_new)], axis=0)


def _cache_update_copies(new_ref, cin, cout_hbm, rows_ref, sems, layer, seq_idx, n_new):
    H = HEADS_PER_GROUP
    R = n_new * H
    copies = []
    for g in range(N_GROUPS):
        for kv, base in ((0, N_GROUPS * H), (1, 2 * N_GROUPS * H)):
            a = 2 * g + kv
            src, dst = cin[g][kv], cout_hbm[g][kv]
            n = src.shape[2]
            rows_ref[a] = _new_rows(new_ref, base, g, n_new)
            copies.append(pltpu.make_async_copy(src.at[0, 0, pl.ds(R, n - R), :],
                                                dst.at[layer, seq_idx, pl.ds(0, n - R), :], sems.at[2 * a]))
            copies.append(pltpu.make_async_copy(rows_ref.at[a],
                                                dst.at[layer, seq_idx, pl.ds(n - R, R), :], sems.at[2 * a + 1]))
    return copies


def _sample_win_body(new_ref, cin, bias_ref, o_ref, kcat_ref, vcat_ref, n_new):
    H = HEADS_PER_GROUP
    R = n_new * H
    NPART = 4
    qb, kb, vb = 0, N_GROUPS * H, 2 * N_GROUPS * H

    rows0 = cin[0][0].shape[0]
    kcat_ref[0:rows0, :] = cin[0][0][...]
    vcat_ref[0:rows0, :] = cin[0][1][...]
    kcat_ref[rows0:rows0 + R, :] = _new_rows(new_ref, kb, 0, n_new)
    vcat_ref[rows0:rows0 + R, :] = _new_rows(new_ref, vb, 0, n_new)

    lane = lax.broadcasted_iota(jnp.int32, (R, BAND), 1)
    src = [(kcat_ref, vcat_ref) if g == 0 else cin[g] for g in range(N_GROUPS)]
    softmaxed = []
    for g, (win, dil) in enumerate(DIL_GROUPS):
        q = _new_rows(new_ref, qb, g, n_new)
        pitch = dil * H
        sc = [jnp.zeros((R, BAND), F32) for _ in range(NPART)]
        for s in range(BAND):
            red = jnp.sum(src[g][0][s * pitch: s * pitch + R, :] * q, axis=-1, keepdims=True)
            sc[s % NPART] = jnp.where(lane == s, red, sc[s % NPART])
        scores = ((sc[0] + sc[1]) + (sc[2] + sc[3])) * SCALE + bias_ref[g, 0]
        s_own = jnp.sum(_new_rows(new_ref, kb, g, n_new) * q, axis=-1, keepdims=True) * SCALE \
            + bias_ref[g, 1][:, 0:1]
        m = jnp.maximum(jnp.max(scores, axis=-1, keepdims=True), s_own)
        p = jnp.exp(scores - m)
        p_own = jnp.exp(s_own - m)
        softmaxed.append((p, p_own, m, jnp.sum(p, axis=-1, keepdims=True) + p_own))
    parts = []
    for g, (win, dil) in enumerate(DIL_GROUPS):
        p, p_own, m, l = softmaxed[g]
        pitch = dil * H
        acc = [jnp.zeros((R, HEAD_DIM), F32) for _ in range(NPART)]
        for s in range(BAND):
            acc[s % NPART] = acc[s % NPART] + p[:, s:s + 1] * src[g][1][s * pitch: s * pitch + R, :]
        parts.append((((acc[0] + acc[1]) + (acc[2] + acc[3])) + p_own * _new_rows(new_ref, vb, g, n_new), m, l))

    m = functools.reduce(jnp.maximum, [pt[1] for pt in parts])
    num = jnp.zeros((R, HEAD_DIM), F32)
    den = jnp.zeros((R, 1), F32)
    for acc, mg, lg in parts:
        w = jnp.exp(mg - m)
        num = num + w * acc
        den = den + w * lg
    o = num / den
    for t in range(n_new):
        o_ref[t] = o[t * H:(t + 1) * H, :]


def _sample_mem_body(q_ref, qblk, k_ref, v_ref, o_ref, n_new):
    pad = jnp.zeros((8 - n_new, HEAD_DIM), F32)
    scores = []
    for h in range(MEM_HEADS):
        q = jnp.concatenate([q_ref[t, qblk + h: qblk + h + 1, :] for t in range(n_new)] + [pad], axis=0)
        scores.append(_dot_nt(q, k_ref[pl.ds(h, N_MEM, stride=MEM_HEADS), :]) * SCALE)
    probs = []
    for s in scores:
        p = jnp.exp(s - jnp.max(s, axis=-1, keepdims=True))
        probs.append((p, jnp.sum(p, axis=-1, keepdims=True)))
    for h, (p, l) in enumerate(probs):
        o = _dot(p, v_ref[pl.ds(h, N_MEM, stride=MEM_HEADS), :]) / l
        for t in range(n_new):
            o_ref[t, h:h + 1, :] = o[t:t + 1, :]


def _gate_terms(ba, alog_row, dtb_row):
    beta = _sigmoid(ba)
    g = -jnp.exp(alog_row) * _softplus(ba + dtb_row)
    return beta, g


def _delta_prompt_body(x_ref, halo_ref, ba_ref, cw_ref, alog_ref, dtb_ref, onw_ref,
                       o_ref, s_out_ref, xs_ref, s_ref):
    C = BLOCK_ROWS
    NH = N_HEADS_B
    it = pl.program_id(1)

    @pl.when(it == 0)
    def _():
        s_ref[...] = jnp.zeros(s_ref.shape, F32)
        xs_ref[0:8, :] = jnp.zeros((8, xs_ref.shape[1]), F32)

    @pl.when(it > 0)
    def _():
        xs_ref[0:8, :] = halo_ref[...]

    xs_ref[8:8 + C, :] = x_ref[...]

    def conv_block(c0):
        cs = slice(c0, c0 + HEAD_DIM)
        acc = xs_ref[pl.ds(8 - (CONV_W - 1), C), cs] * cw_ref[0:1, cs]
        for wi in range(1, CONV_W):
            acc = acc + xs_ref[pl.ds(8 - (CONV_W - 1) + wi, C), cs] * cw_ref[wi:wi + 1, cs]
        return _silu(acc)

    beta, g = _gate_terms(ba_ref[...], alog_ref[...], dtb_ref[...])
    row = lax.broadcasted_iota(jnp.int32, (C, C), 0)
    col = lax.broadcasted_iota(jnp.int32, (C, C), 1)
    tri = row >= col
    strict = row > col
    ones_tri = jnp.where(tri, 1.0, 0.0).astype(BF16)
    gh, gm, gl = _split3(g)
    gc = _mm(ones_tri, gh) + (_mm(ones_tri, gm) + _mm(ones_tri, gl))
    gc_t = gc.T
    eye = jnp.where(row == col, 1.0, 0.0)

    q, k, v, kb, egc, decay, b_col, gc_col = [], [], [], [], [], [], [], []
    m_hi, m_lo, tinv = [], [], []
    for h in range(NH):
        q.append(_l2norm(conv_block(h * HEAD_DIM)) * SCALE)
        k.append(_l2norm(conv_block(WIDTH_B + h * HEAD_DIM)))
        v.append(conv_block(2 * WIDTH_B + h * HEAD_DIM))
        b_col.append(beta[:, h:h + 1])
        gc_col.append(gc[:, 8 + h: 9 + h])
        gc_row = gc_t[8 + h: 9 + h, :]
        decay.append(jnp.where(tri, jnp.exp(jnp.where(tri, gc_col[h] - gc_row, 0.0)), 0.0))
        kb.append(k[h] * b_col[h])
        egc.append(jnp.exp(gc_col[h]))
        lmat = jnp.where(strict, _dot_nt(kb[h], k[h]) * decay[h], 0.0)
        hi, lo = _split2(-lmat)
        m_hi.append(hi)
        m_lo.append(lo)
        tinv.append(eye - lmat)

    for _ in range(int(math.log2(C)) - 1):
        for h in range(NH):
            sq = _mm(m_hi[h], m_hi[h]) + (_mm(m_hi[h], m_lo[h]) + _mm(m_lo[h], m_hi[h]))
            m_hi[h], m_lo[h] = _split2(sq)
        for h in range(NH):
            t_hi, t_lo = _split2(tinv[h])
            tinv[h] = tinv[h] + (_mm(t_hi, m_hi[h]) + (_mm(t_hi, m_lo[h]) + _mm(t_lo, m_hi[h])))

    uw = [_dot(tinv[h], jnp.concatenate([v[h] * b_col[h], kb[h] * egc[h]], axis=1)) for h in range(NH)]
    aqk = [_dot_nt(q[h], k[h]) * decay[h] for h in range(NH)]
    for h in range(NH):
        hs = slice(h * HEAD_DIM, (h + 1) * HEAD_DIM)
        g_last = gc_col[h][C - 1:C, :]
        kdec = k[h] * jnp.exp(g_last - gc_col[h])
        s_prev = s_ref[h]
        rs = _dot(jnp.concatenate([uw[h][:, HEAD_DIM:], q[h] * egc[h]], axis=0), s_prev)
        v_new = uw[h][:, :HEAD_DIM] - rs[0:C]
        o = rs[C:2 * C] + _dot(aqk[h], v_new)
        s_ref[h] = s_prev * jnp.exp(g_last) + _dot(kdec.T, v_new)
        ms = jnp.mean(o * o, axis=-1, keepdims=True)
        o_ref[:, hs] = o * lax.rsqrt(ms + EPS) * onw_ref[...]

    @pl.when(it == pl.num_programs(1) - 1)
    def _():
        s_out_ref[...] = s_ref[...]


N_DP_IN = 7


def _prompt_delta_sample_win_kernel(*refs, n_new, n_alias, win_layer):
    dp_in = refs[:N_DP_IN]
    new_ref = refs[N_DP_IN]
    c = refs[N_DP_IN + 1: N_DP_IN + 7]
    bias_ref, memk_ref, memv_ref = refs[N_DP_IN + 7: N_DP_IN + 10]
    outs = refs[N_DP_IN + 10 + n_alias:]
    o_ref, s_out_ref = outs[0:2]
    co = outs[2:8]
    os_ref, om_ref = outs[8:10]
    xs_ref, s_ref, kcat_ref, vcat_ref, rows_ref, sems = outs[10:16]
    pairs = lambda r: ((r[0], r[1]), (r[2], r[3]), (r[4], r[5]))
    cin = pairs([x.at[0, 0] for x in c])
    seq_idx = pl.program_id(0) * pl.num_programs(1) + pl.program_id(1)
    copies = _cache_update_copies(new_ref, pairs(c), pairs(co), rows_ref, sems, win_layer, seq_idx, n_new)
    for cp in copies:
        cp.start()
    _delta_prompt_body(*dp_in, o_ref, s_out_ref, xs_ref, s_ref)
    _sample_win_body(new_ref, cin, bias_ref, os_ref, kcat_ref, vcat_ref, n_new)
    _sample_mem_body(new_ref, 4 * 9, memk_ref, memv_ref, om_ref, n_new)
    for cp in copies:
        cp.wait()


def _prompt_delta_sample_win(proj_p, conv_w, alog_row, dtb_row, onw, bp, seq,
                             proj_s, caches, win_layer, prev_out, bias_s, mem_k, mem_v, mem_layer, bs, n_new):
    C = BLOCK_ROWS
    H = HEADS_PER_GROUP
    steps = seq // C
    assert bs == bp * steps and n_new * H % 8 == 0
    for g, (win, dil) in enumerate(DIL_GROUPS):
        assert caches[2 * g].shape[2] == win * H and win // dil == BAND
    wq = 3 * WIDTH_B
    sb = lambda b, i: b * steps + i
    const2 = lambda shape: pl.BlockSpec(shape, lambda b, i: (0, 0))
    new = proj_s.reshape(bs, n_new, IN_A // 128, 128)
    cspecs = [pl.BlockSpec((1, 1) + c.shape[2:], lambda b, i: (win_layer, sb(b, i), 0, 0)) for c in caches]
    mspec = pl.BlockSpec((None, None, N_MEM * MEM_HEADS, HEAD_DIM), lambda b, i: (mem_layer, sb(b, i), 0, 0))
    rows0 = caches[0].shape[2]
    alias_args = list(prev_out) if prev_out is not None else []
    n_alias = len(alias_args)
    n_in = N_DP_IN + 10
    small = lambda w: pl.BlockSpec((None, n_new, H, w), lambda b, i: (sb(b, i), 0, 0, 0))
    hbm = pl.BlockSpec(memory_space=pl.ANY)
    outs = pl.pallas_call(
        functools.partial(_prompt_delta_sample_win_kernel, n_new=n_new, n_alias=n_alias, win_layer=win_layer),
        grid=(bp, steps),
        in_specs=[pl.BlockSpec((C, wq), lambda b, i: (sb(b, i), 0)),
                  pl.BlockSpec((8, wq), lambda b, i: (jnp.maximum(sb(b, i) * (C // 8) - 1, 0), 0)),
                  pl.BlockSpec((C, 128), lambda b, i: (sb(b, i), BA_COL_BLOCK)),
                  const2((CONV_W, wq)), const2((1, 128)), const2((1, 128)), const2((1, HEAD_DIM)),
                  pl.BlockSpec((None, n_new, IN_A // 128, 128), lambda b, i: (sb(b, i), 0, 0, 0))]
                 + cspecs
                 + [pl.BlockSpec(bias_s.shape, lambda b, i: (0, 0, 0, 0)), mspec, mspec]
                 + [hbm] * n_alias,
        out_specs=[pl.BlockSpec((C, WIDTH_B), lambda b, i: (sb(b, i), 0)),
                   pl.BlockSpec((None, N_HEADS_B, HEAD_DIM, HEAD_DIM), lambda b, i: (b, 0, 0, 0))]
                  + [hbm] * len(caches) + [small(HEAD_DIM), small(HEAD_DIM)],
        out_shape=[jax.ShapeDtypeStruct((bp * seq, WIDTH_B), F32),
                   jax.ShapeDtypeStruct((bp, N_HEADS_B, HEAD_DIM, HEAD_DIM), F32)]
                  + [jax.ShapeDtypeStruct(c.shape, F32) for c in caches]
                  + [jax.ShapeDtypeStruct((bs, n_new, H, HEAD_DIM), F32),
                     jax.ShapeDtypeStruct((bs, n_new, MEM_HEADS, HEAD_DIM), F32)],
        scratch_shapes=[pltpu.VMEM((8 + C, wq), F32),
                        pltpu.VMEM((N_HEADS_B, HEAD_DIM, HEAD_DIM), F32),
                        pltpu.VMEM((rows0 + n_new * H, 128), F32),
                        pltpu.VMEM((rows0 + n_new * H, 128), F32),
                        pltpu.VMEM((len(caches), n_new * H, HEAD_DIM), F32),
                        pltpu.SemaphoreType.DMA((2 * len(caches),))],
        input_output_aliases={n_in + k: 2 + k for k in range(n_alias)},
        compiler_params=_cparams(("parallel", "arbitrary")),
        name="prompt_delta_sample_win",
    )(proj_p, proj_p, proj_p, conv_w, alog_row, dtb_row, onw, new, *caches, bias_s, mem_k, mem_v, *alias_args)
    o_p, s_p = outs[0], outs[1]
    return (o_p, s_p, outs[2:8], outs[8].reshape(bs * n_new, WIDTH_A_OUT), outs[9].reshape(bs * n_new, WIDTH_MEM))


def _delta_sample_kernel(*refs, n_new, n_alias):
    NH = N_HEADS_B
    x_ref, cs_ref, s_in_ref, cw_ref, alog_ref, dtb_ref, onw_ref, memk_ref, memv_ref = refs[:9]
    o_ref, s_out_ref, om_ref, st_ref, rs_ref, kd_ref, vn_ref = refs[9 + n_alias:]
    qkv_rows = 3 * NH

    _sample_mem_body(x_ref, 4 * 9, memk_ref, memv_ref, om_ref, n_new)

    xp = [cs_ref[i] for i in range(CONV_W - 1)] + [x_ref[t, 0:qkv_rows, :] for t in range(n_new)]
    ba = jnp.concatenate([x_ref[t, BA_COL_BLOCK:BA_COL_BLOCK + 1, :] for t in range(n_new)]
                         + [jnp.zeros((HEAD_DIM - n_new, 128), F32)], axis=0)
    ba_t = ba.T
    alog_col = alog_ref[:, 0:1]
    dtb_col = dtb_ref[:, 0:1]

    q, k, v, beta, gc = [], [], [], [], []
    run = jnp.zeros((NH, 1), F32)
    for t in range(n_new):
        acc = xp[t] * cw_ref[0]
        for wi in range(1, CONV_W):
            acc = acc + xp[t + wi] * cw_ref[wi]
        conv = _silu(acc)
        q.append(_l2norm(conv[0:NH]) * SCALE)
        k.append(_l2norm(conv[NH:2 * NH]))
        v.append(conv[2 * NH:3 * NH])
        beta.append(_sigmoid(ba_t[0:NH, t:t + 1]))
        run = run - jnp.exp(alog_col) * _softplus(ba_t[NH:2 * NH, t:t + 1] + dtb_col)
        gc.append(run)
    egc = [jnp.exp(x) for x in gc]
    dec = {(i, j): jnp.exp(gc[i] - gc[j]) for i in range(n_new) for j in range(i)}
    lsum = lambda a, b: jnp.sum(a * b, axis=-1, keepdims=True)

    u, w = [], []
    for i in range(n_new):
        ui = v[i] * beta[i]
        wi_ = k[i] * (beta[i] * egc[i])
        for j in range(i):
            lij = beta[i] * lsum(k[i], k[j]) * dec[i, j]
            ui = ui - lij * u[j]
            wi_ = wi_ - lij * w[j]
        u.append(ui)
        w.append(wi_)

    for i in range(n_new):
        st_ref[i * NH:(i + 1) * NH, :] = w[i]
        st_ref[(n_new + i) * NH:(n_new + i + 1) * NH, :] = q[i] * egc[i]
    for h in range(NH):
        rs_ref[h * 8:(h + 1) * 8, :] = _dot(st_ref[pl.ds(h, 2 * n_new, stride=NH), :], s_in_ref[h])

    vn, g_last = [], gc[n_new - 1]
    for i in range(n_new):
        vn.append(u[i] - rs_ref[pl.ds(i, NH, stride=8), :])
    for i in range(n_new):
        o = rs_ref[pl.ds(n_new + i, NH, stride=8), :] + lsum(q[i], k[i]) * vn[i]
        for j in range(i):
            o = o + (lsum(q[i], k[j]) * dec[i, j]) * vn[j]
        ms = jnp.mean(o * o, axis=-1, keepdims=True)
        o_ref[i] = o * lax.rsqrt(ms + EPS) * onw_ref[...]

    kd_ref[...] = jnp.zeros(kd_ref.shape, F32)
    vn_ref[...] = jnp.zeros(vn_ref.shape, F32)
    for i in range(n_new):
        kd_ref[i * NH:(i + 1) * NH, :] = k[i] * jnp.exp(g_last - gc[i])
        vn_ref[i * NH:(i + 1) * NH, :] = vn[i]
    eg_last = jnp.exp(g_last)
    zpad = jnp.zeros((HEAD_DIM - 8, HEAD_DIM), F32)
    for h in range(NH):
        kd = jnp.concatenate([kd_ref[pl.ds(h, 8, stride=NH), :], zpad], axis=0)
        vh = jnp.concatenate([vn_ref[pl.ds(h, 8, stride=NH), :], zpad], axis=0)
        s_out_ref[h] = s_in_ref[h] * eg_last[h:h + 1, :] + _dot(kd.T, vh)


def _delta_sample(proj_s, conv_state, delta_state, layer, prev_out, conv_w, a_log, dt_bias, onw,
                  mem_k, mem_v, mem_layer, batch, n_new):
    assert n_new == 4 and N_HEADS_B == 8
    ncol = IN_B_PAD // 128
    new = proj_s.reshape(batch, n_new, ncol, 128)
    cw = conv_w.reshape(CONV_W, 3 * N_HEADS_B, HEAD_DIM)
    alog = jnp.broadcast_to(a_log[:, None], (N_HEADS_B, 128))
    dtb = jnp.broadcast_to(dt_bias[:, None], (N_HEADS_B, 128))
    sspec = pl.BlockSpec((None, None, N_HEADS_B, HEAD_DIM, HEAD_DIM), lambda b: (layer, b, 0, 0, 0))
    mspec = pl.BlockSpec((None, None, N_MEM * MEM_HEADS, HEAD_DIM), lambda b: (mem_layer, b, 0, 0))
    alias_args = [prev_out] if prev_out is not None else []
    n_alias = len(alias_args)
    n_in = 9
    o, s, om = pl.pallas_call(
        functools.partial(_delta_sample_kernel, n_new=n_new, n_alias=n_alias),
        grid=(batch,),
        in_specs=[pl.BlockSpec((None, n_new, ncol, 128), lambda b: (b, 0, 0, 0)),
                  pl.BlockSpec((None, None, CONV_W - 1, 3 * N_HEADS_B, HEAD_DIM), lambda b: (layer, b, 0, 0, 0)),
                  sspec,
                  pl.BlockSpec(cw.shape, lambda b: (0, 0, 0)),
                  pl.BlockSpec((N_HEADS_B, 128), lambda b: (0, 0)),
                  pl.BlockSpec((N_HEADS_B, 128), lambda b: (0, 0)),
                  pl.BlockSpec((1, HEAD_DIM), lambda b: (0, 0)),
                  mspec, mspec]
                 + [pl.BlockSpec(memory_space=pl.ANY)] * n_alias,
        out_specs=[pl.BlockSpec((None, n_new, N_HEADS_B, HEAD_DIM), lambda b: (b, 0, 0, 0)), sspec,
                   pl.BlockSpec((None, n_new, MEM_HEADS, HEAD_DIM), lambda b: (b, 0, 0, 0))],
        out_shape=[jax.ShapeDtypeStruct((batch, n_new, N_HEADS_B, HEAD_DIM), F32),
                   jax.ShapeDtypeStruct(delta_state.shape, F32),
                   jax.ShapeDtypeStruct((batch, n_new, MEM_HEADS, HEAD_DIM), F32)],
        scratch_shapes=[pltpu.VMEM((2 * n_new * N_HEADS_B, HEAD_DIM), F32),
                        pltpu.VMEM((8 * N_HEADS_B, HEAD_DIM), F32),
                        pltpu.VMEM((8 * N_HEADS_B, HEAD_DIM), F32),
                        pltpu.VMEM((8 * N_HEADS_B, HEAD_DIM), F32)],
        input_output_aliases={n_in + k: 1 + k for k in range(n_alias)},
        compiler_params=_cparams(("parallel",)),
        name="delta_sample",
    )(new, conv_state, delta_state, cw, alog, dtb, onw, mem_k, mem_v, *alias_args)
    return o.reshape(batch * n_new, WIDTH_B), s, om.reshape(batch * n_new, WIDTH_MEM)


def _reorder_w_in_b(w):
    q0 = 3 * WIDTH_B
    b0, a0, m0, z0 = q0, q0 + N_HEADS_B, q0 + 2 * N_HEADS_B, q0 + 2 * N_HEADS_B + WIDTH_MEM
    parts = [w[:, :q0], w[:, z0:], w[:, m0:z0], w[:, b0:a0], w[:, a0:m0]]
    out = jnp.concatenate(parts, axis=1)
    return jnp.pad(out, ((0, 0), (0, IN_B_PAD - out.shape[1])))


def _lane_row(vals, offset):
    return jnp.zeros((1, 128), F32).at[0, offset:offset + vals.shape[0]].set(vals)


def kernel(x_prompt, x_sample, cache_win_k0, cache_win_v0, cache_win_k1, cache_win_v1, cache_win_k2, cache_win_v2, state_conv, state_delta, cache_mem_k, cache_mem_v, mem_prompt, norm_w, final_norm_w, rel_bias, w_in_a, w_out_a, w_in_b, conv_w, a_log, dt_bias, o_norm_w, w_out_b, w_mem_kv):
    bp, seq, d = x_prompt.shape
    bs, n_new, _ = x_sample.shape
    depth = norm_w.shape[0]
    assert depth % 2 == 0
    H = HEADS_PER_GROUP
    tm_p = 1024
    ms = bs * n_new

    xp = x_prompt.reshape(bp * seq, d)
    xs = x_sample.reshape(ms, d)

    mk_all, mv_all = _mem_kv(mem_prompt.reshape(bp * N_MEM, d), w_mem_kv.astype(BF16))
    bias_p = _bias_prompt(rel_bias)
    bias_s = _bias_sample(rel_bias, n_new)
    mem_k_s = cache_mem_k.reshape(depth, bs, N_MEM * MEM_HEADS, HEAD_DIM)
    mem_v_s = cache_mem_v.reshape(depth, bs, N_MEM * MEM_HEADS, HEAD_DIM)
    caches_in = (cache_win_k0, cache_win_v0, cache_win_k1, cache_win_v1, cache_win_k2, cache_win_v2)
    caches_flat = [c.reshape(c.shape[0], bs, c.shape[2] * H, HEAD_DIM) for c in caches_in]
    conv_state = state_conv.reshape(state_conv.shape[0], bs, CONV_W - 1, 3 * N_HEADS_B, HEAD_DIM)

    p_win = [[] for _ in range(2 * N_GROUPS)]
    p_conv, p_delta, s_conv = [], [], []
    s_win, s_delta = None, None

    for li in range(depth // 2):
        ia, ib = 2 * li, 2 * li + 1
        w_in = w_in_a[li].astype(BF16)
        w_out_even = w_out_a[li].astype(BF16)
        nw = norm_w[ia].reshape(1, d)
        proj_p = _norm_proj(xp, nw, w_in, 2 * tm_p, 1024)
        proj_s_even = _norm_proj(xs, nw, w_in, ms, 2048)
        p3 = proj_p.reshape(bp, seq, IN_A)
        for g, (win, dil) in enumerate(DIL_GROUPS):
            keep = min(win, seq)
            kcol = WIDTH_A_QKV + g * WIDTH_A_OUT
            vcol = 2 * WIDTH_A_QKV + g * WIDTH_A_OUT
            p_win[2 * g].append(p3[:, seq - keep:, kcol:kcol + WIDTH_A_OUT].reshape(bp, keep, H, HEAD_DIM))
            p_win[2 * g + 1].append(p3[:, seq - keep:, vcol:vcol + WIDTH_A_OUT].reshape(bp, keep, H, HEAD_DIM))
        o_p = _prompt_attn(proj_p, bias_p, bp, seq)
        om_p = _mem_attn(proj_p, 9, mk_all, mv_all, ia, bp, seq, 512)
        xp = _out_proj(o_p, om_p, proj_p, 10, 11, xp, w_out_even[:WIDTH_A_OUT], w_out_even[WIDTH_A_OUT:], 512,
                       in_place=li > 0)

        w_in = _reorder_w_in_b(w_in_b[li]).astype(BF16)
        w_out = w_out_b[li].astype(BF16)
        onw = o_norm_w[li].reshape(1, HEAD_DIM)
        nw = norm_w[ib].reshape(1, d)
        wq = 3 * WIDTH_B
        proj_p = _norm_proj(xp, nw, w_in, tm_p, 1792)
        o_p, s_p, s_win, o_s, om_s = _prompt_delta_sample_win(
            proj_p, conv_w[li], _lane_row(a_log[li], N_HEADS_B), _lane_row(dt_bias[li], N_HEADS_B), onw, bp, seq,
            proj_s_even, caches_flat, li, s_win, bias_s, mem_k_s, mem_v_s, ia, bs, n_new)
        p_delta.append(s_p)
        p_conv.append(proj_p.reshape(bp, seq, IN_B_PAD)[:, seq - (CONV_W - 1):, :wq])
        om_p = _mem_attn(proj_p, 9, mk_all, mv_all, ib, bp, seq, 512)
        xp = _out_proj(o_p, om_p, proj_p, 3, 8, xp, w_out[:WIDTH_B], w_out[WIDTH_B:], 512, in_place=True)
        xs = _out_proj(o_s, om_s, proj_s_even, 10, 11, xs, w_out_even[:WIDTH_A_OUT], w_out_even[WIDTH_A_OUT:], ms,
                       in_place=li > 0)

        proj_s = _norm_proj(xs, nw, w_in, ms, 1792)
        o_s, s_delta, om_s = _delta_sample(proj_s, conv_state, state_delta, li, s_delta, conv_w[li],
                                           a_log[li], dt_bias[li], onw, mem_k_s, mem_v_s, ib, bs, n_new)
        xcat = jnp.concatenate([state_conv[li], proj_s.reshape(bs, n_new, IN_B_PAD)[:, :, :wq]], axis=1)
        s_conv.append(xcat[:, n_new:])
        xs = _out_proj(o_s, om_s, proj_s, 3, 8, xs, w_out[:WIDTH_B], w_out[WIDTH_B:], ms, in_place=True)

    fnw = final_norm_w.reshape(1, d)
    y_prompt = _final_norm(xp, fnw, tm_p).reshape(bp, seq, d)
    y_sample = _final_norm(xs, fnw, ms).reshape(bs, n_new, d)
    p_mk = mk_all.reshape(depth, bp, N_MEM, MEM_HEADS, HEAD_DIM)
    p_mv = mv_all.reshape(depth, bp, N_MEM, MEM_HEADS, HEAD_DIM)
    s_win = [s_win[n].reshape(caches_in[n].shape) for n in range(2 * N_GROUPS)]
    return (y_prompt, y_sample,
            jnp.stack(p_win[0]), jnp.stack(p_win[1]), jnp.stack(p_win[2]),
            jnp.stack(p_win[3]), jnp.stack(p_win[4]), jnp.stack(p_win[5]),
            jnp.stack(p_conv), jnp.stack(p_delta), p_mk, p_mv,
            s_win[0], s_win[1], s_win[2], s_win[3], s_win[4], s_win[5],
            jnp.stack(s_conv), s_delta)
```

```python
import functools
import math

import jax
import jax.numpy as jnp
import numpy as np
from jax import lax
from jax.experimental import pallas as pl
from jax.experimental.pallas import tpu as pltpu

F32 = jnp.float32
BF16 = jnp.bfloat16

D_MODEL = 1024
HEAD_DIM = 128
DIL_GROUPS = ((128, 1), (512, 4), (2048, 16))
N_GROUPS = len(DIL_GROUPS)
HEADS_PER_GROUP = 4
WIDTH_A_QKV = N_GROUPS * HEADS_PER_GROUP * HEAD_DIM
WIDTH_A_OUT = HEADS_PER_GROUP * HEAD_DIM
N_MEM = 256
MEM_HEADS = 4
WIDTH_MEM = MEM_HEADS * HEAD_DIM
N_HEADS_B = 8
WIDTH_B = N_HEADS_B * HEAD_DIM
CONV_W = 4
N_BUCKETS = 32
MAX_EXACT = N_BUCKETS // 2
MAX_DIST = 2048
EPS = 1e-6
NEG = -1e30
SCALE = HEAD_DIM ** -0.5
BAND = 128
BLOCK_ROWS = 128
ATTN_TILE = BAND * max(d for _, d in DIL_GROUPS)
ATTN_BLOCKS_IN_FLIGHT = 4
IN_A = 3 * WIDTH_A_QKV + WIDTH_MEM + WIDTH_A_OUT + WIDTH_MEM
IN_B_PAD = 5376
BA_COL_BLOCK = (3 * WIDTH_B + WIDTH_B + WIDTH_MEM + WIDTH_MEM) // 128
VMEM_LIMIT = 48 * 1024 * 1024


def _cparams(sem):
    return pltpu.CompilerParams(dimension_semantics=sem, vmem_limit_bytes=VMEM_LIMIT)


def _sigmoid(x):
    return 1.0 / (1.0 + jnp.exp(-x))


def _silu(x):
    return x * _sigmoid(x)


def _softplus(x):
    return jnp.maximum(x, 0.0) + jnp.log(1.0 + jnp.exp(-jnp.abs(x)))


def _mm(a, b):
    return jnp.dot(a, b, preferred_element_type=F32)


def _dot(a, b):
    return _mm(a.astype(BF16), b.astype(BF16))


def _dot_nt(a, b):
    return lax.dot_general(a.astype(BF16), b.astype(BF16), (((1,), (1,)), ((), ())),
                           preferred_element_type=F32)


def _split2(a):
    hi = a.astype(BF16)
    lo = (a - hi.astype(F32)).astype(BF16)
    return hi, lo


def _split3(a):
    hi = a.astype(BF16)
    r1 = a - hi.astype(F32)
    mid = r1.astype(BF16)
    lo = (r1 - mid.astype(F32)).astype(BF16)
    return hi, mid, lo


def _l2norm(x):
    return x * lax.rsqrt(jnp.sum(x * x, axis=-1, keepdims=True) + EPS)


def _norm_proj_kernel(x_ref, nw_ref, w_ref, o_ref, h_ref):
    @pl.when(pl.program_id(1) == 0)
    def _():
        x = x_ref[...]
        ms = jnp.mean(x * x, axis=-1, keepdims=True)
        h_ref[...] = (x * lax.rsqrt(ms + EPS) * nw_ref[...]).astype(BF16)

    o_ref[...] = _mm(h_ref[...], w_ref[...])


def _norm_proj(x, nw, w, tm, tn):
    m, d = x.shape
    n = w.shape[1]
    return pl.pallas_call(
        _norm_proj_kernel,
        grid=(m // tm, n // tn),
        in_specs=[pl.BlockSpec((tm, d), lambda i, j: (i, 0)),
                  pl.BlockSpec((1, d), lambda i, j: (0, 0)),
                  pl.BlockSpec((d, tn), lambda i, j: (0, j))],
        out_specs=pl.BlockSpec((tm, tn), lambda i, j: (i, j)),
        out_shape=jax.ShapeDtypeStruct((m, n), F32),
        scratch_shapes=[pltpu.VMEM((tm, d), BF16)],
        compiler_params=_cparams(("parallel", "arbitrary")),
        name="norm_proj",
    )(x, nw, w)


def _final_norm_kernel(x_ref, nw_ref, o_ref):
    x = x_ref[...]
    ms = jnp.mean(x * x, axis=-1, keepdims=True)
    o_ref[...] = x * lax.rsqrt(ms + EPS) * nw_ref[...]


def _final_norm(x, nw, tm):
    m, d = x.shape
    return pl.pallas_call(
        _final_norm_kernel,
        grid=(m // tm,),
        in_specs=[pl.BlockSpec((tm, d), lambda i: (i, 0)),
                  pl.BlockSpec((1, d), lambda i: (0, 0))],
        out_specs=pl.BlockSpec((tm, d), lambda i: (i, 0)),
        out_shape=jax.ShapeDtypeStruct((m, d), F32),
        compiler_params=_cparams(("parallel",)),
        name="final_norm",
    )(x, nw)


def _mem_kv_kernel(x_ref, w_ref, k_ref, v_ref):
    r = _mm(x_ref[...].astype(BF16), w_ref[...])
    k_ref[...] = r[:, :WIDTH_MEM]
    v_ref[...] = r[:, WIDTH_MEM:]


def _mem_kv(mem, w):
    m, d = mem.shape
    depth = w.shape[0]
    out = jax.ShapeDtypeStruct((depth, m, WIDTH_MEM), F32)
    return pl.pallas_call(
        _mem_kv_kernel,
        grid=(depth,),
        in_specs=[pl.BlockSpec((m, d), lambda i: (0, 0)),
                  pl.BlockSpec((None, d, 2 * WIDTH_MEM), lambda i: (i, 0, 0))],
        out_specs=[pl.BlockSpec((None, m, WIDTH_MEM), lambda i: (i, 0, 0)),
                   pl.BlockSpec((None, m, WIDTH_MEM), lambda i: (i, 0, 0))],
        out_shape=[out, out],
        compiler_params=_cparams(("parallel",)),
        name="mem_kv",
    )(mem, w)


def _rel_bucket_np(dist):
    dist = np.asarray(dist, np.int64)
    df = np.maximum(dist, 1).astype(np.float32)
    large = MAX_EXACT + (np.log(df / np.float32(MAX_EXACT)) / np.float32(math.log(MAX_DIST / MAX_EXACT))
                         * np.float32(N_BUCKETS - MAX_EXACT)).astype(np.int32)
    return np.where(dist < MAX_EXACT, dist, np.minimum(large, N_BUCKETS - 1)).astype(np.int32)


def _bias_prompt_kernel(bkt_ref, tab_ref, o_ref):
    g = pl.program_id(0)
    bkt = bkt_ref[...]
    accs = [jnp.zeros(bkt.shape, F32) for _ in range(HEADS_PER_GROUP)]
    for b in range(N_BUCKETS):
        hit = bkt == b
        for h in range(HEADS_PER_GROUP):
            accs[h] = jnp.where(hit, tab_ref[b, g * HEADS_PER_GROUP + h], accs[h])
    for h in range(HEADS_PER_GROUP):
        o_ref[h] = accs[h]


def _bias_prompt(rel_bias):
    i = np.arange(BAND)[:, None]
    j = np.arange(2 * BAND)[None, :]
    dsub = np.maximum(i + BAND - j, 0)
    bkt = np.stack([_rel_bucket_np(dsub * dil) for _, dil in DIL_GROUPS])
    return pl.pallas_call(
        _bias_prompt_kernel,
        grid=(N_GROUPS,),
        in_specs=[pl.BlockSpec((None, BAND, 2 * BAND), lambda g: (g, 0, 0)),
                  pl.BlockSpec(memory_space=pltpu.SMEM)],
        out_specs=pl.BlockSpec((None, HEADS_PER_GROUP, BAND, 2 * BAND), lambda g: (g, 0, 0, 0)),
        out_shape=jax.ShapeDtypeStruct((N_GROUPS, HEADS_PER_GROUP, BAND, 2 * BAND), F32),
        compiler_params=_cparams(("arbitrary",)),
        name="bias_prompt",
    )(jnp.asarray(bkt), rel_bias)


def _bias_sample_kernel(bkt_ref, tab_ref, o_ref, *, n_new):
    g = pl.program_id(0)
    H = HEADS_PER_GROUP
    bkt = bkt_ref[...]
    for h in range(H):
        c = g * H + h
        row = jnp.zeros(bkt.shape, F32)
        for b in range(N_BUCKETS):
            row = jnp.where(bkt == b, tab_ref[b, c], row)
        own = jnp.zeros(bkt.shape, F32) + tab_ref[0, c]
        for t in range(n_new):
            o_ref[0, t * H + h: t * H + h + 1, :] = row
            o_ref[1, t * H + h: t * H + h + 1, :] = own


def _bias_sample(rel_bias, n_new):
    rows = n_new * HEADS_PER_GROUP
    bkt = np.stack([_rel_bucket_np((BAND - np.arange(BAND)) * dil) for _, dil in DIL_GROUPS])[:, None, :]
    return pl.pallas_call(
        functools.partial(_bias_sample_kernel, n_new=n_new),
        grid=(N_GROUPS,),
        in_specs=[pl.BlockSpec((None, 1, BAND), lambda g: (g, 0, 0)),
                  pl.BlockSpec(memory_space=pltpu.SMEM)],
        out_specs=pl.BlockSpec((None, 2, rows, BAND), lambda g: (g, 0, 0, 0)),
        out_shape=jax.ShapeDtypeStruct((N_GROUPS, 2, rows, BAND), F32),
        compiler_params=_cparams(("arbitrary",)),
        name="bias_sample",
    )(jnp.asarray(bkt), rel_bias)


def _prompt_attn_kernel(*refs):
    ins, (o_ref, acc_ref, ml_ref) = refs[:-3], refs[-3:]
    first_tile = pl.program_id(1) == 0
    row = lax.broadcasted_iota(jnp.int32, (BAND, BAND), 0)
    col = lax.broadcasted_iota(jnp.int32, (BAND, BAND), 1)
    keep_cur = col <= row
    keep_prev = col >= row
    keep_prev_first = keep_prev & jnp.logical_not(first_tile)

    blocks = [(g, n, r) for g, (_, dil) in enumerate(DIL_GROUPS)
              for n in range(ATTN_TILE // (BAND * dil)) for r in range(dil)]
    for first in range(0, len(blocks), ATTN_BLOCKS_IN_FLIGHT):
        staged = []
        for g, n, r in blocks[first: first + ATTN_BLOCKS_IN_FLIGHT]:
            dil = DIL_GROUPS[g][1]
            q_ref, kc_ref, vc_ref, kp_ref, vp_ref, bias_ref = ins[6 * g: 6 * g + 6]
            cur = pl.ds(n * BAND * dil + r, BAND, stride=dil)
            q = q_ref[cur, :].astype(BF16)
            kc = kc_ref[cur, :]
            if n == 0:
                prev_src, prev, mask_prev = (kp_ref, vp_ref), pl.ds(r, BAND, stride=dil), keep_prev_first
            else:
                prev_src, prev, mask_prev = (kc_ref, vc_ref), pl.ds((n - 1) * BAND * dil + r, BAND, stride=dil), keep_prev
            s_c = jnp.where(keep_cur, _dot_nt(q, kc) * SCALE + bias_ref[:, BAND:2 * BAND], NEG)
            s_p = jnp.where(mask_prev, _dot_nt(q, prev_src[0][prev, :]) * SCALE + bias_ref[:, 0:BAND], NEG)
            staged.append((g, cur, vc_ref, prev_src[1], prev, s_c, s_p))
        probs = []
        for g, cur, vc_ref, vp_src, prev, s_c, s_p in staged:
            m = jnp.maximum(jnp.max(s_c, axis=-1, keepdims=True), jnp.max(s_p, axis=-1, keepdims=True))
            p_c = jnp.exp(s_c - m)
            p_p = jnp.exp(s_p - m)
            l = jnp.sum(p_c, axis=-1, keepdims=True) + jnp.sum(p_p, axis=-1, keepdims=True)
            probs.append((p_c, p_p, m, l))
        for (g, cur, vc_ref, vp_src, prev, _, _), (p_c, p_p, m, l) in zip(staged, probs):
            acc_ref[g, cur, :] = _dot(p_c, vc_ref[cur, :]) + _dot(p_p, vp_src[prev, :])
            ml_ref[g, cur, :] = m
            ml_ref[N_GROUPS + g, cur, :] = l

    chunk = 256
    for c in range(ATTN_TILE // chunk):
        rows = slice(c * chunk, (c + 1) * chunk)
        ms = [ml_ref[g, rows, :] for g in range(N_GROUPS)]
        m = functools.reduce(jnp.maximum, ms)
        num = jnp.zeros((chunk, HEAD_DIM), F32)
        den = jnp.zeros((chunk, 1), F32)
        for g in range(N_GROUPS):
            w = jnp.exp(ms[g] - m)
            num = num + w * acc_ref[g, rows, :]
            den = den + w * ml_ref[N_GROUPS + g, rows, :]
        o_ref[rows, :] = num / den


def _prompt_attn(proj, bias_p, batch, seq):
    H = HEADS_PER_GROUP
    assert seq % ATTN_TILE == 0
    ntile = seq // ATTN_TILE
    args, specs = [], []
    for g, (win, dil) in enumerate(DIL_GROUPS):
        assert win // dil == BAND
        span = BAND * dil
        per = ATTN_TILE // span
        qc, kc, vc = g * H, (N_GROUPS + g) * H, (2 * N_GROUPS + g) * H
        cur = lambda c: (lambda b, i, h: (b * ntile + i, c + h))
        prev = lambda c, per=per: (lambda b, i, h: (jnp.maximum((b * ntile + i) * per - 1, 0), c + h))
        specs += [pl.BlockSpec((ATTN_TILE, HEAD_DIM), cur(qc)),
                  pl.BlockSpec((ATTN_TILE, HEAD_DIM), cur(kc)),
                  pl.BlockSpec((ATTN_TILE, HEAD_DIM), cur(vc)),
                  pl.BlockSpec((span, HEAD_DIM), prev(kc)),
                  pl.BlockSpec((span, HEAD_DIM), prev(vc)),
                  pl.BlockSpec((None, None, BAND, 2 * BAND), lambda b, i, h, g=g: (g, h, 0, 0))]
        args += [proj] * 5 + [bias_p]
    return pl.pallas_call(
        _prompt_attn_kernel,
        grid=(batch, ntile, H),
        in_specs=specs,
        out_specs=pl.BlockSpec((ATTN_TILE, HEAD_DIM), lambda b, i, h: (b * ntile + i, h)),
        out_shape=jax.ShapeDtypeStruct((batch * seq, WIDTH_A_OUT), F32),
        scratch_shapes=[pltpu.VMEM((N_GROUPS, ATTN_TILE, HEAD_DIM), F32),
                        pltpu.VMEM((2 * N_GROUPS, ATTN_TILE, 1), F32)],
        compiler_params=_cparams(("parallel", "arbitrary", "arbitrary")),
        name="prompt_attn",
    )(*args)


def _export_win_kernel(*refs, n_alias):
    k_ref, v_ref = refs[0:2]
    ok_ref, ov_ref = refs[2 + n_alias: 4 + n_alias]
    H = HEADS_PER_GROUP
    rows = k_ref.shape[0]
    for src, dst in ((k_ref, ok_ref), (v_ref, ov_ref)):
        for h in range(H):
            dst[pl.ds(h, rows, stride=H), :] = src[:, h * HEAD_DIM:(h + 1) * HEAD_DIM]


def _export_win(proj, g, layer, n_layers, prev_out, batch, seq):
    H = HEADS_PER_GROUP
    keep = min(DIL_GROUPS[g][0], seq)
    tile = min(keep, 256)
    assert keep % tile == 0 and seq % tile == 0
    steps = keep // tile
    first = (seq - keep) // tile
    kc = WIDTH_A_QKV // WIDTH_A_OUT + g
    vc = 2 * WIDTH_A_QKV // WIDTH_A_OUT + g
    ospec = pl.BlockSpec((None, None, tile * H, HEAD_DIM), lambda b, i: (layer, b, i, 0))
    alias_args = list(prev_out) if prev_out is not None else []
    n_alias = len(alias_args)
    shape = jax.ShapeDtypeStruct((n_layers, batch, keep * H, HEAD_DIM), F32)
    return pl.pallas_call(
        functools.partial(_export_win_kernel, n_alias=n_alias),
        grid=(batch, steps),
        in_specs=[pl.BlockSpec((tile, WIDTH_A_OUT), lambda b, i: (b * (seq // tile) + first + i, kc)),
                  pl.BlockSpec((tile, WIDTH_A_OUT), lambda b, i: (b * (seq // tile) + first + i, vc))]
                 + [pl.BlockSpec(memory_space=pl.ANY)] * n_alias,
        out_specs=[ospec, ospec],
        out_shape=[shape, shape],
        input_output_aliases={2 + k: k for k in range(n_alias)},
        compiler_params=_cparams(("parallel", "parallel")),
        name=f"export_win_g{g}",
    )(proj, proj, *alias_args)


def _mem_attn_kernel(q_ref, k_ref, v_ref, o_ref):
    for h in range(MEM_HEADS):
        hs = slice(h * HEAD_DIM, (h + 1) * HEAD_DIM)
        s = _dot_nt(q_ref[:, hs], k_ref[:, hs]) * SCALE
        m = jnp.max(s, axis=-1, keepdims=True)
        p = jnp.exp(s - m)
        l = jnp.sum(p, axis=-1, keepdims=True)
        o_ref[:, hs] = _dot(p, v_ref[:, hs]) / l


def _mem_attn(proj, qcol, mk, mv, layer, batch, seq, tm):
    steps = seq // tm
    return pl.pallas_call(
        _mem_attn_kernel,
        grid=(batch, steps),
        in_specs=[pl.BlockSpec((tm, WIDTH_MEM), lambda b, i: (b * steps + i, qcol)),
                  pl.BlockSpec((None, N_MEM, WIDTH_MEM), lambda b, i: (layer, b, 0)),
                  pl.BlockSpec((None, N_MEM, WIDTH_MEM), lambda b, i: (layer, b, 0))],
        out_specs=pl.BlockSpec((tm, WIDTH_MEM), lambda b, i: (b * steps + i, 0)),
        out_shape=jax.ShapeDtypeStruct((batch * seq, WIDTH_MEM), F32),
        compiler_params=_cparams(("parallel", "parallel")),
        name="mem_attn",
    )(proj, mk, mv)


def _out_proj_kernel(a1_ref, a2_ref, z1_ref, z2_ref, x_ref, w1_ref, w2_ref, o_ref):
    g1 = (a1_ref[...] * _silu(z1_ref[...])).astype(BF16)
    g2 = (a2_ref[...] * _silu(z2_ref[...])).astype(BF16)
    y = _mm(g1, w1_ref[...])
    y = y + _mm(g2, w2_ref[...])
    o_ref[...] = x_ref[...] + y


def _out_proj(a1, a2, proj, z1col, z2col, x, w1, w2, tm, in_place):
    m, d = x.shape
    k1, k2 = a1.shape[1], a2.shape[1]
    return pl.pallas_call(
        _out_proj_kernel,
        grid=(m // tm,),
        in_specs=[pl.BlockSpec((tm, k1), lambda i: (i, 0)),
                  pl.BlockSpec((tm, k2), lambda i: (i, 0)),
                  pl.BlockSpec((tm, k1), lambda i: (i, z1col)),
                  pl.BlockSpec((tm, k2), lambda i: (i, z2col)),
                  pl.BlockSpec((tm, d), lambda i: (i, 0)),
                  pl.BlockSpec((k1, d), lambda i: (0, 0)),
                  pl.BlockSpec((k2, d), lambda i: (0, 0))],
        out_specs=pl.BlockSpec((tm, d), lambda i: (i, 0)),
        out_shape=jax.ShapeDtypeStruct((m, d), F32),
        input_output_aliases={4: 0} if in_place else {},
        compiler_params=_cparams(("parallel",)),
        name="out_proj",
    )(a1, a2, proj, proj, x, w1, w2)


def _new_rows(new_ref, base, g, n_new):
    H = HEADS_PER_GROUP
    return jnp.concatenate([new_ref[t, base + g * H: base + (g + 1) * H, :] for t in range(n_new)], axis=0)


def _cache_update_copies(new_ref, cin, cout_hbm, rows_ref, sems, layer, seq_idx, n_new):
    H = HEADS_PER_GROUP
    R = n_new * H
    copies = []
    for g in range(N_GROUPS):
        for kv, base in ((0, N_GROUPS * H), (1, 2 * N_GROUPS * H)):
            a = 2 * g + kv
            src, dst = cin[g][kv], cout_hbm[g][kv]
            n = src.shape[2]
            rows_ref[a] = _new_rows(new_ref, base, g, n_new)
            copies.append(pltpu.make_async_copy(src.at[0, 0, pl.ds(R, n - R), :],
                                                dst.at[layer, seq_idx, pl.ds(0, n - R), :], sems.at[2 * a]))
            copies.append(pltpu.make_async_copy(rows_ref.at[a],
                                                dst.at[layer, seq_idx, pl.ds(n - R, R), :], sems.at[2 * a + 1]))
    return copies


def _sample_win_body(new_ref, cin, bias_ref, o_ref, kcat_ref, vcat_ref, n_new):
    H = HEADS_PER_GROUP
    R = n_new * H
    NPART = 4
    qb, kb, vb = 0, N_GROUPS * H, 2 * N_GROUPS * H

    rows0 = cin[0][0].shape[0]
    kcat_ref[0:rows0, :] = cin[0][0][...]
    vcat_ref[0:rows0, :] = cin[0][1][...]
    kcat_ref[rows0:rows0 + R, :] = _new_rows(new_ref, kb, 0, n_new)
    vcat_ref[rows0:rows0 + R, :] = _new_rows(new_ref, vb, 0, n_new)

    lane = lax.broadcasted_iota(jnp.int32, (R, BAND), 1)
    src = [(kcat_ref, vcat_ref) if g == 0 else cin[g] for g in range(N_GROUPS)]
    softmaxed = []
    for g, (win, dil) in enumerate(DIL_GROUPS):
        q = _new_rows(new_ref, qb, g, n_new)
        pitch = dil * H
        sc = [jnp.zeros((R, BAND), F32) for _ in range(NPART)]
        for s in range(BAND):
            red = jnp.sum(src[g][0][s * pitch: s * pitch + R, :] * q, axis=-1, keepdims=True)
            sc[s % NPART] = jnp.where(lane == s, red, sc[s % NPART])
        scores = ((sc[0] + sc[1]) + (sc[2] + sc[3])) * SCALE + bias_ref[g, 0]
        s_own = jnp.sum(_new_rows(new_ref, kb, g, n_new) * q, axis=-1, keepdims=True) * SCALE \
            + bias_ref[g, 1][:, 0:1]
        m = jnp.maximum(jnp.max(scores, axis=-1, keepdims=True), s_own)
        p = jnp.exp(scores - m)
        p_own = jnp.exp(s_own - m)
        softmaxed.append((p, p_own, m, jnp.sum(p, axis=-1, keepdims=True) + p_own))
    parts = []
    for g, (win, dil) in enumerate(DIL_GROUPS):
        p, p_own, m, l = softmaxed[g]
        pitch = dil * H
        acc = [jnp.zeros((R, HEAD_DIM), F32) for _ in range(NPART)]
        for s in range(BAND):
            acc[s % NPART] = acc[s % NPART] + p[:, s:s + 1] * src[g][1][s * pitch: s * pitch + R, :]
        parts.append((((acc[0] + acc[1]) + (acc[2] + acc[3])) + p_own * _new_rows(new_ref, vb, g, n_new), m, l))

    m = functools.reduce(jnp.maximum, [pt[1] for pt in parts])
    num = jnp.zeros((R, HEAD_DIM), F32)
    den = jnp.zeros((R, 1), F32)
    for acc, mg, lg in parts:
        w = jnp.exp(mg - m)
        num = num + w * acc
        den = den + w * lg
    o = num / den
    for t in range(n_new):
        o_ref[t] = o[t * H:(t + 1) * H, :]


def _sample_mem_body(q_ref, qblk, k_ref, v_ref, o_ref, n_new):
    pad = jnp.zeros((8 - n_new, HEAD_DIM), F32)
    scores = []
    for h in range(MEM_HEADS):
        q = jnp.concatenate([q_ref[t, qblk + h: qblk + h + 1, :] for t in range(n_new)] + [pad], axis=0)
        scores.append(_dot_nt(q, k_ref[pl.ds(h, N_MEM, stride=MEM_HEADS), :]) * SCALE)
    probs = []
    for s in scores:
        p = jnp.exp(s - jnp.max(s, axis=-1, keepdims=True))
        probs.append((p, jnp.sum(p, axis=-1, keepdims=True)))
    for h, (p, l) in enumerate(probs):
        o = _dot(p, v_ref[pl.ds(h, N_MEM, stride=MEM_HEADS), :]) / l
        for t in range(n_new):
            o_ref[t, h:h + 1, :] = o[t:t + 1, :]


def _gate_terms(ba, alog_row, dtb_row):
    beta = _sigmoid(ba)
    g = -jnp.exp(alog_row) * _softplus(ba + dtb_row)
    return beta, g


def _delta_prompt_body(x_ref, halo_ref, ba_ref, cw_ref, alog_ref, dtb_ref, onw_ref,
                       o_ref, xs_ref, s_ref):
    C = BLOCK_ROWS
    NH = N_HEADS_B
    it = pl.program_id(1)

    @pl.when(it == 0)
    def _():
        s_ref[...] = jnp.zeros(s_ref.shape, F32)
        xs_ref[0:8, :] = jnp.zeros((8, xs_ref.shape[1]), F32)

    @pl.when(it > 0)
    def _():
        xs_ref[0:8, :] = halo_ref[...]

    xs_ref[8:8 + C, :] = x_ref[...]

    def conv_block(c0):
        cs = slice(c0, c0 + HEAD_DIM)
        acc = xs_ref[pl.ds(8 - (CONV_W - 1), C), cs] * cw_ref[0:1, cs]
        for wi in range(1, CONV_W):
            acc = acc + xs_ref[pl.ds(8 - (CONV_W - 1) + wi, C), cs] * cw_ref[wi:wi + 1, cs]
        return _silu(acc)

    beta, g = _gate_terms(ba_ref[...], alog_ref[...], dtb_ref[...])
    row = lax.broadcasted_iota(jnp.int32, (C, C), 0)
    col = lax.broadcasted_iota(jnp.int32, (C, C), 1)
    tri = row >= col
    strict = row > col
    ones_tri = jnp.where(tri, 1.0, 0.0).astype(BF16)
    gh, gm, gl = _split3(g)
    gc = _mm(ones_tri, gh) + (_mm(ones_tri, gm) + _mm(ones_tri, gl))
    gc_t = gc.T
    eye = jnp.where(row == col, 1.0, 0.0)

    q, k, v, kb, egc, decay, b_col, gc_col = [], [], [], [], [], [], [], []
    m_hi, m_lo, tinv = [], [], []
    for h in range(NH):
        q.append(_l2norm(conv_block(h * HEAD_DIM)) * SCALE)
        k.append(_l2norm(conv_block(WIDTH_B + h * HEAD_DIM)))
        v.append(conv_block(2 * WIDTH_B + h * HEAD_DIM))
        b_col.append(beta[:, h:h + 1])
        gc_col.append(gc[:, 8 + h: 9 + h])
        gc_row = gc_t[8 + h: 9 + h, :]
        decay.append(jnp.where(tri, jnp.exp(jnp.where(tri, gc_col[h] - gc_row, 0.0)), 0.0))
        kb.append(k[h] * b_col[h])
        egc.append(jnp.exp(gc_col[h]))
        lmat = jnp.where(strict, _dot_nt(kb[h], k[h]) * decay[h], 0.0)
        hi, lo = _split2(-lmat)
        m_hi.append(hi)
        m_lo.append(lo)
        tinv.append(eye - lmat)

    for _ in range(int(math.log2(C)) - 1):
        for h in range(NH):
            sq = _mm(m_hi[h], m_hi[h]) + (_mm(m_hi[h], m_lo[h]) + _mm(m_lo[h], m_hi[h]))
            m_hi[h], m_lo[h] = _split2(sq)
        for h in range(NH):
            t_hi, t_lo = _split2(tinv[h])
            tinv[h] = tinv[h] + (_mm(t_hi, m_hi[h]) + (_mm(t_hi, m_lo[h]) + _mm(t_lo, m_hi[h])))

    uw = [_dot(tinv[h], jnp.concatenate([v[h] * b_col[h], kb[h] * egc[h]], axis=1)) for h in range(NH)]
    aqk = [_dot_nt(q[h], k[h]) * decay[h] for h in range(NH)]
    for h in range(NH):
        hs = slice(h * HEAD_DIM, (h + 1) * HEAD_DIM)
        g_last = gc_col[h][C - 1:C, :]
        kdec = k[h] * jnp.exp(g_last - gc_col[h])
        s_prev = s_ref[h]
        rs = _dot(jnp.concatenate([uw[h][:, HEAD_DIM:], q[h] * egc[h]], axis=0), s_prev)
        v_new = uw[h][:, :HEAD_DIM] - rs[0:C]
        o = rs[C:2 * C] + _dot(aqk[h], v_new)
        s_ref[h] = s_prev * jnp.exp(g_last) + _dot(kdec.T, v_new)
        ms = jnp.mean(o * o, axis=-1, keepdims=True)
        o_ref[:, hs] = o * lax.rsqrt(ms + EPS) * onw_ref[...]


N_DP_IN = 7


def _prompt_delta_sample_win_kernel(*refs, n_new, n_alias, win_layer):
    dp_in = refs[:N_DP_IN]
    new_ref = refs[N_DP_IN]
    c = refs[N_DP_IN + 1: N_DP_IN + 7]
    bias_ref, memk_ref, memv_ref = refs[N_DP_IN + 7: N_DP_IN + 10]
    outs = refs[N_DP_IN + 10 + n_alias:]
    o_ref, s_out_ref = outs[0:2]
    co = outs[2:8]
    os_ref, om_ref = outs[8:10]
    xs_ref, s_ref, kcat_ref, vcat_ref, rows_ref, sems = outs[10:16]
    pairs = lambda r: ((r[0], r[1]), (r[2], r[3]), (r[4], r[5]))
    cin = pairs([x.at[0, 0] for x in c])
    seq_idx = pl.program_id(0) * pl.num_programs(1) + pl.program_id(1)
    copies = _cache_update_copies(new_ref, pairs(c), pairs(co), rows_ref, sems, win_layer, seq_idx, n_new)
    for cp in copies:
        cp.start()
    _delta_prompt_body(*dp_in, o_ref, xs_ref, s_ref)
    _sample_win_body(new_ref, cin, bias_ref, os_ref, kcat_ref, vcat_ref, n_new)
    _sample_mem_body(new_ref, 4 * 9, memk_ref, memv_ref, om_ref, n_new)
    for cp in copies:
        cp.wait()

    @pl.when(pl.program_id(1) == pl.num_programs(1) - 1)
    def _():
        s_out_ref[...] = s_ref[...]


def _prompt_delta_sample_win(proj_p, conv_w, alog_row, dtb_row, onw, bp, seq,
                             proj_s, caches, win_layer, prev_out, bias_s, mem_k, mem_v, mem_layer, bs, n_new):
    C = BLOCK_ROWS
    H = HEADS_PER_GROUP
    steps = seq // C
    assert bs == bp * steps and n_new * H % 8 == 0
    for g, (win, dil) in enumerate(DIL_GROUPS):
        assert caches[2 * g].shape[2] == win * H and win // dil == BAND
    wq = 3 * WIDTH_B
    sb = lambda b, i: b * steps + i
    const2 = lambda shape: pl.BlockSpec(shape, lambda b, i: (0, 0))
    new = proj_s.reshape(bs, n_new, IN_A // 128, 128)
    cspecs = [pl.BlockSpec((1, 1) + c.shape[2:], lambda b, i: (win_layer, sb(b, i), 0, 0)) for c in caches]
    mspec = pl.BlockSpec((None, None, N_MEM * MEM_HEADS, HEAD_DIM), lambda b, i: (mem_layer, sb(b, i), 0, 0))
    rows0 = caches[0].shape[2]
    alias_args = list(prev_out) if prev_out is not None else []
    n_alias = len(alias_args)
    n_in = N_DP_IN + 10
    small = lambda w: pl.BlockSpec((None, n_new, H, w), lambda b, i: (sb(b, i), 0, 0, 0))
    hbm = pl.BlockSpec(memory_space=pl.ANY)
    outs = pl.pallas_call(
        functools.partial(_prompt_delta_sample_win_kernel, n_new=n_new, n_alias=n_alias, win_layer=win_layer),
        grid=(bp, steps),
        in_specs=[pl.BlockSpec((C, wq), lambda b, i: (sb(b, i), 0)),
                  pl.BlockSpec((8, wq), lambda b, i: (jnp.maximum(sb(b, i) * (C // 8) - 1, 0), 0)),
                  pl.BlockSpec((C, 128), lambda b, i: (sb(b, i), BA_COL_BLOCK)),
                  const2((CONV_W, wq)), const2((1, 128)), const2((1, 128)), const2((1, HEAD_DIM)),
                  pl.BlockSpec((None, n_new, IN_A // 128, 128), lambda b, i: (sb(b, i), 0, 0, 0))]
                 + cspecs
                 + [pl.BlockSpec(bias_s.shape, lambda b, i: (0, 0, 0, 0)), mspec, mspec]
                 + [hbm] * n_alias,
        out_specs=[pl.BlockSpec((C, WIDTH_B), lambda b, i: (sb(b, i), 0)),
                   pl.BlockSpec((None, N_HEADS_B, HEAD_DIM, HEAD_DIM), lambda b, i: (b, 0, 0, 0))]
                  + [hbm] * len(caches) + [small(HEAD_DIM), small(HEAD_DIM)],
        out_shape=[jax.ShapeDtypeStruct((bp * seq, WIDTH_B), F32),
                   jax.ShapeDtypeStruct((bp, N_HEADS_B, HEAD_DIM, HEAD_DIM), F32)]
                  + [jax.ShapeDtypeStruct(c.shape, F32) for c in caches]
                  + [jax.ShapeDtypeStruct((bs, n_new, H, HEAD_DIM), F32),
                     jax.ShapeDtypeStruct((bs, n_new, MEM_HEADS, HEAD_DIM), F32)],
        scratch_shapes=[pltpu.VMEM((8 + C, wq), F32),
                        pltpu.VMEM((N_HEADS_B, HEAD_DIM, HEAD_DIM), F32),
                        pltpu.VMEM((rows0 + n_new * H, 128), F32),
                        pltpu.VMEM((rows0 + n_new * H, 128), F32),
                        pltpu.VMEM((len(caches), n_new * H, HEAD_DIM), F32),
                        pltpu.SemaphoreType.DMA((2 * len(caches),))],
        input_output_aliases={n_in + k: 2 + k for k in range(n_alias)},
        compiler_params=_cparams(("parallel", "arbitrary")),
        name="prompt_delta_sample_win",
    )(proj_p, proj_p, proj_p, conv_w, alog_row, dtb_row, onw, new, *caches, bias_s, mem_k, mem_v, *alias_args)
    o_p, s_p = outs[0], outs[1]
    return (o_p, s_p, outs[2:8], outs[8].reshape(bs * n_new, WIDTH_A_OUT), outs[9].reshape(bs * n_new, WIDTH_MEM))


DELTA_SAMPLE_SEQS = 2


def _delta_sample_kernel(*refs, n_new, n_alias):
    x_ref, cs_ref, s_in_ref, cw_ref, alog_ref, dtb_ref, onw_ref, memk_ref, memv_ref = refs[:9]
    o_ref, s_out_ref, om_ref, st_ref, rs_ref, kd_ref, vn_ref = refs[9 + n_alias:]
    for s in range(DELTA_SAMPLE_SEQS):
        _delta_sample_body(x_ref.at[s], cs_ref.at[0, s], s_in_ref.at[0, s], cw_ref, alog_ref, dtb_ref, onw_ref,
                           memk_ref.at[0, s], memv_ref.at[0, s], o_ref.at[s], s_out_ref.at[0, s], om_ref.at[s],
                           st_ref.at[s], rs_ref.at[s], kd_ref.at[s], vn_ref.at[s], n_new)


def _delta_sample_body(x_ref, cs_ref, s_in_ref, cw_ref, alog_ref, dtb_ref, onw_ref, memk_ref, memv_ref,
                       o_ref, s_out_ref, om_ref, st_ref, rs_ref, kd_ref, vn_ref, n_new):
    NH = N_HEADS_B
    qkv_rows = 3 * NH

    _sample_mem_body(x_ref, 4 * 9, memk_ref, memv_ref, om_ref, n_new)

    xp = [cs_ref[i] for i in range(CONV_W - 1)] + [x_ref[t, 0:qkv_rows, :] for t in range(n_new)]
    ba = jnp.concatenate([x_ref[t, BA_COL_BLOCK:BA_COL_BLOCK + 1, :] for t in range(n_new)]
                         + [jnp.zeros((HEAD_DIM - n_new, 128), F32)], axis=0)
    ba_t = ba.T
    alog_col = alog_ref[:, 0:1]
    dtb_col = dtb_ref[:, 0:1]

    q, k, v, beta, gc = [], [], [], [], []
    run = jnp.zeros((NH, 1), F32)
    for t in range(n_new):
        acc = xp[t] * cw_ref[0]
        for wi in range(1, CONV_W):
            acc = acc + xp[t + wi] * cw_ref[wi]
        conv = _silu(acc)
        q.append(_l2norm(conv[0:NH]) * SCALE)
        k.append(_l2norm(conv[NH:2 * NH]))
        v.append(conv[2 * NH:3 * NH])
        beta.append(_sigmoid(ba_t[0:NH, t:t + 1]))
        run = run - jnp.exp(alog_col) * _softplus(ba_t[NH:2 * NH, t:t + 1] + dtb_col)
        gc.append(run)
    egc = [jnp.exp(x) for x in gc]
    dec = {(i, j): jnp.exp(gc[i] - gc[j]) for i in range(n_new) for j in range(i)}
    lsum = lambda a, b: jnp.sum(a * b, axis=-1, keepdims=True)

    u, w = [], []
    for i in range(n_new):
        ui = v[i] * beta[i]
        wi_ = k[i] * (beta[i] * egc[i])
        for j in range(i):
            lij = beta[i] * lsum(k[i], k[j]) * dec[i, j]
            ui = ui - lij * u[j]
            wi_ = wi_ - lij * w[j]
        u.append(ui)
        w.append(wi_)

    for i in range(n_new):
        st_ref[i * NH:(i + 1) * NH, :] = w[i]
        st_ref[(n_new + i) * NH:(n_new + i + 1) * NH, :] = q[i] * egc[i]
    for h in range(NH):
        rs_ref[h * 8:(h + 1) * 8, :] = _dot(st_ref[pl.ds(h, 2 * n_new, stride=NH), :], s_in_ref[h])

    vn, g_last = [], gc[n_new - 1]
    for i in range(n_new):
        vn.append(u[i] - rs_ref[pl.ds(i, NH, stride=8), :])
    for i in range(n_new):
        o = rs_ref[pl.ds(n_new + i, NH, stride=8), :] + lsum(q[i], k[i]) * vn[i]
        for j in range(i):
            o = o + (lsum(q[i], k[j]) * dec[i, j]) * vn[j]
        ms = jnp.mean(o * o, axis=-1, keepdims=True)
        o_ref[i] = o * lax.rsqrt(ms + EPS) * onw_ref[...]

    kd_ref[...] = jnp.zeros(kd_ref.shape, F32)
    vn_ref[...] = jnp.zeros(vn_ref.shape, F32)
    for i in range(n_new):
        kd_ref[i * NH:(i + 1) * NH, :] = k[i] * jnp.exp(g_last - gc[i])
        vn_ref[i * NH:(i + 1) * NH, :] = vn[i]
    eg_last = jnp.exp(g_last)
    zpad = jnp.zeros((HEAD_DIM - 8, HEAD_DIM), F32)
    for h in range(NH):
        kd = jnp.concatenate([kd_ref[pl.ds(h, 8, stride=NH), :], zpad], axis=0)
        vh = jnp.concatenate([vn_ref[pl.ds(h, 8, stride=NH), :], zpad], axis=0)
        s_out_ref[h] = s_in_ref[h] * eg_last[h:h + 1, :] + _dot(kd.T, vh)


def _delta_sample(proj_s, conv_state, delta_state, layer, prev_out, conv_w, a_log, dt_bias, onw,
                  mem_k, mem_v, mem_layer, batch, n_new):
    assert n_new == 4 and N_HEADS_B == 8
    ncol = IN_B_PAD // 128
    new = proj_s.reshape(batch, n_new, ncol, 128)
    cw = conv_w.reshape(CONV_W, 3 * N_HEADS_B, HEAD_DIM)
    alog = jnp.broadcast_to(a_log[:, None], (N_HEADS_B, 128))
    dtb = jnp.broadcast_to(dt_bias[:, None], (N_HEADS_B, 128))
    S = DELTA_SAMPLE_SEQS
    assert batch % S == 0
    sspec = pl.BlockSpec((1, S, N_HEADS_B, HEAD_DIM, HEAD_DIM), lambda b: (layer, b, 0, 0, 0))
    mspec = pl.BlockSpec((1, S, N_MEM * MEM_HEADS, HEAD_DIM), lambda b: (mem_layer, b, 0, 0))
    alias_args = [prev_out] if prev_out is not None else []
    n_alias = len(alias_args)
    n_in = 9
    o, s, om = pl.pallas_call(
        functools.partial(_delta_sample_kernel, n_new=n_new, n_alias=n_alias),
        grid=(batch // S,),
        in_specs=[pl.BlockSpec((S, n_new, ncol, 128), lambda b: (b, 0, 0, 0)),
                  pl.BlockSpec((1, S, CONV_W - 1, 3 * N_HEADS_B, HEAD_DIM), lambda b: (layer, b, 0, 0, 0)),
                  sspec,
                  pl.BlockSpec(cw.shape, lambda b: (0, 0, 0)),
                  pl.BlockSpec((N_HEADS_B, 128), lambda b: (0, 0)),
                  pl.BlockSpec((N_HEADS_B, 128), lambda b: (0, 0)),
                  pl.BlockSpec((1, HEAD_DIM), lambda b: (0, 0)),
                  mspec, mspec]
                 + [pl.BlockSpec(memory_space=pl.ANY)] * n_alias,
        out_specs=[pl.BlockSpec((S, n_new, N_HEADS_B, HEAD_DIM), lambda b: (b, 0, 0, 0)), sspec,
                   pl.BlockSpec((S, n_new, MEM_HEADS, HEAD_DIM), lambda b: (b, 0, 0, 0))],
        out_shape=[jax.ShapeDtypeStruct((batch, n_new, N_HEADS_B, HEAD_DIM), F32),
                   jax.ShapeDtypeStruct(delta_state.shape, F32),
                   jax.ShapeDtypeStruct((batch, n_new, MEM_HEADS, HEAD_DIM), F32)],
        scratch_shapes=[pltpu.VMEM((S, 2 * n_new * N_HEADS_B, HEAD_DIM), F32),
                        pltpu.VMEM((S, 8 * N_HEADS_B, HEAD_DIM), F32),
                        pltpu.VMEM((S, 8 * N_HEADS_B, HEAD_DIM), F32),
                        pltpu.VMEM((S, 8 * N_HEADS_B, HEAD_DIM), F32)],
        input_output_aliases={n_in + k: 1 + k for k in range(n_alias)},
        compiler_params=_cparams(("parallel",)),
        name="delta_sample",
    )(new, conv_state, delta_state, cw, alog, dtb, onw, mem_k, mem_v, *alias_args)
    return o.reshape(batch * n_new, WIDTH_B), s, om.reshape(batch * n_new, WIDTH_MEM)


def _reorder_w_in_b(w):
    q0 = 3 * WIDTH_B
    b0, a0, m0, z0 = q0, q0 + N_HEADS_B, q0 + 2 * N_HEADS_B, q0 + 2 * N_HEADS_B + WIDTH_MEM
    parts = [w[:, :q0], w[:, z0:], w[:, m0:z0], w[:, b0:a0], w[:, a0:m0]]
    out = jnp.concatenate(parts, axis=1)
    return jnp.pad(out, ((0, 0), (0, IN_B_PAD - out.shape[1])))


def _lane_row(vals, offset):
    return jnp.zeros((1, 128), F32).at[0, offset:offset + vals.shape[0]].set(vals)


def kernel(x_prompt, x_sample, cache_win_k0, cache_win_v0, cache_win_k1, cache_win_v1, cache_win_k2, cache_win_v2, state_conv, state_delta, cache_mem_k, cache_mem_v, mem_prompt, norm_w, final_norm_w, rel_bias, w_in_a, w_out_a, w_in_b, conv_w, a_log, dt_bias, o_norm_w, w_out_b, w_mem_kv):
    bp, seq, d = x_prompt.shape
    bs, n_new, _ = x_sample.shape
    depth = norm_w.shape[0]
    assert depth % 2 == 0
    H = HEADS_PER_GROUP
    tm_p = 1024
    ms = bs * n_new

    xp = x_prompt.reshape(bp * seq, d)
    xs = x_sample.reshape(ms, d)

    mk_all, mv_all = _mem_kv(mem_prompt.reshape(bp * N_MEM, d), w_mem_kv.astype(BF16))
    bias_p = _bias_prompt(rel_bias)
    bias_s = _bias_sample(rel_bias, n_new)
    mem_k_s = cache_mem_k.reshape(depth, bs, N_MEM * MEM_HEADS, HEAD_DIM)
    mem_v_s = cache_mem_v.reshape(depth, bs, N_MEM * MEM_HEADS, HEAD_DIM)
    caches_in = (cache_win_k0, cache_win_v0, cache_win_k1, cache_win_v1, cache_win_k2, cache_win_v2)
    caches_flat = [c.reshape(c.shape[0], bs, c.shape[2] * H, HEAD_DIM) for c in caches_in]
    conv_state = state_conv.reshape(state_conv.shape[0], bs, CONV_W - 1, 3 * N_HEADS_B, HEAD_DIM)

    p_win = [None] * N_GROUPS
    p_conv, p_delta, s_conv = [], [], []
    s_win, s_delta = None, None

    for li in range(depth // 2):
        ia, ib = 2 * li, 2 * li + 1
        w_in = w_in_a[li].astype(BF16)
        w_out_even = w_out_a[li].astype(BF16)
        nw = norm_w[ia].reshape(1, d)
        proj_p = _norm_proj(xp, nw, w_in, 2 * tm_p, 1024)
        proj_s_even = _norm_proj(xs, nw, w_in, ms, 2048)
        for g in range(N_GROUPS):
            p_win[g] = _export_win(proj_p, g, li, depth // 2, p_win[g], bp, seq)
        o_p = _prompt_attn(proj_p, bias_p, bp, seq)
        om_p = _mem_attn(proj_p, 9, mk_all, mv_all, ia, bp, seq, 512)
        xp = _out_proj(o_p, om_p, proj_p, 10, 11, xp, w_out_even[:WIDTH_A_OUT], w_out_even[WIDTH_A_OUT:], 512,
                       in_place=li > 0)

        w_in = _reorder_w_in_b(w_in_b[li]).astype(BF16)
        w_out = w_out_b[li].astype(BF16)
        onw = o_norm_w[li].reshape(1, HEAD_DIM)
        nw = norm_w[ib].reshape(1, d)
        wq = 3 * WIDTH_B
        proj_p = _norm_proj(xp, nw, w_in, tm_p, 1792)
        o_p, s_p, s_win, o_s, om_s = _prompt_delta_sample_win(
            proj_p, conv_w[li], _lane_row(a_log[li], N_HEADS_B), _lane_row(dt_bias[li], N_HEADS_B), onw, bp, seq,
            proj_s_even, caches_flat, li, s_win, bias_s, mem_k_s, mem_v_s, ia, bs, n_new)
        p_delta.append(s_p)
        p_conv.append(proj_p.reshape(bp, seq, IN_B_PAD)[:, seq - (CONV_W - 1):, :wq])
        om_p = _mem_attn(proj_p, 9, mk_all, mv_all, ib, bp, seq, 512)
        xp = _out_proj(o_p, om_p, proj_p, 3, 8, xp, w_out[:WIDTH_B], w_out[WIDTH_B:], 512, in_place=True)
        xs = _out_proj(o_s, om_s, proj_s_even, 10, 11, xs, w_out_even[:WIDTH_A_OUT], w_out_even[WIDTH_A_OUT:], ms,
                       in_place=li > 0)

        proj_s = _norm_proj(xs, nw, w_in, ms, 1792)
        o_s, s_delta, om_s = _delta_sample(proj_s, conv_state, state_delta, li, s_delta, conv_w[li],
                                           a_log[li], dt_bias[li], onw, mem_k_s, mem_v_s, ib, bs, n_new)
        xcat = jnp.concatenate([state_conv[li], proj_s.reshape(bs, n_new, IN_B_PAD)[:, :, :wq]], axis=1)
        s_conv.append(xcat[:, n_new:])
        xs = _out_proj(o_s, om_s, proj_s, 3, 8, xs, w_out[:WIDTH_B], w_out[WIDTH_B:], ms, in_place=True)

    fnw = final_norm_w.reshape(1, d)
    y_prompt = _final_norm(xp, fnw, tm_p).reshape(bp, seq, d)
    y_sample = _final_norm(xs, fnw, ms).reshape(bs, n_new, d)
    p_mk = mk_all.reshape(depth, bp, N_MEM, MEM_HEADS, HEAD_DIM)
    p_mv = mv_all.reshape(depth, bp, N_MEM, MEM_HEADS, HEAD_DIM)
    s_win = [s_win[n].reshape(caches_in[n].shape) for n in range(2 * N_GROUPS)]
    p_win = [w.reshape(w.shape[0], bp, w.shape[2] // H, H, HEAD_DIM) for pair in p_win for w in pair]
    return (y_prompt, y_sample,
            p_win[0], p_win[1], p_win[2], p_win[3], p_win[4], p_win[5],
            jnp.stack(p_conv), jnp.stack(p_delta), p_mk, p_mv,
            s_win[0], s_win[1], s_win[2], s_win[3], s_win[4], s_win[5],
            jnp.stack(s_conv), s_delta)
```

```python
import functools
import math

import jax
import jax.numpy as jnp
import numpy as np
from jax import lax
from jax.experimental import pallas as pl
from jax.experimental.pallas import tpu as pltpu

F32 = jnp.float32
BF16 = jnp.bfloat16

D_MODEL = 1024
HEAD_DIM = 128
DIL_GROUPS = ((128, 1), (512, 4), (2048, 16))
N_GROUPS = len(DIL_GROUPS)
HEADS_PER_GROUP = 4
WIDTH_A_QKV = N_GROUPS * HEADS_PER_GROUP * HEAD_DIM
WIDTH_A_OUT = HEADS_PER_GROUP * HEAD_DIM
N_MEM = 256
MEM_HEADS = 4
WIDTH_MEM = MEM_HEADS * HEAD_DIM
N_HEADS_B = 8
WIDTH_B = N_HEADS_B * HEAD_DIM
CONV_W = 4
N_BUCKETS = 32
MAX_EXACT = N_BUCKETS // 2
MAX_DIST = 2048
EPS = 1e-6
NEG = -1e30
SCALE = HEAD_DIM ** -0.5
BAND = 128
BLOCK_ROWS = 128
ATTN_TILE = BAND * max(d for _, d in DIL_GROUPS)
ATTN_BLOCKS_IN_FLIGHT = 4
IN_A = 3 * WIDTH_A_QKV + WIDTH_MEM + WIDTH_A_OUT + WIDTH_MEM
IN_B_PAD = 5376
BA_COL_BLOCK = (3 * WIDTH_B + WIDTH_B + WIDTH_MEM + WIDTH_MEM) // 128
VMEM_LIMIT = 48 * 1024 * 1024


def _cparams(sem):
    return pltpu.CompilerParams(dimension_semantics=sem, vmem_limit_bytes=VMEM_LIMIT)


def _sigmoid(x):
    return 1.0 / (1.0 + jnp.exp(-x))


def _silu(x):
    return x * _sigmoid(x)


def _softplus(x):
    return jnp.maximum(x, 0.0) + jnp.log(1.0 + jnp.exp(-jnp.abs(x)))


def _mm(a, b):
    return jnp.dot(a, b, preferred_element_type=F32)


def _dot(a, b):
    return _mm(a.astype(BF16), b.astype(BF16))


def _dot_nt(a, b):
    return lax.dot_general(a.astype(BF16), b.astype(BF16), (((1,), (1,)), ((), ())),
                           preferred_element_type=F32)


def _split2(a):
    hi = a.astype(BF16)
    lo = (a - hi.astype(F32)).astype(BF16)
    return hi, lo


def _split3(a):
    hi = a.astype(BF16)
    r1 = a - hi.astype(F32)
    mid = r1.astype(BF16)
    lo = (r1 - mid.astype(F32)).astype(BF16)
    return hi, mid, lo


def _l2norm(x):
    return x * lax.rsqrt(jnp.sum(x * x, axis=-1, keepdims=True) + EPS)


def _norm_proj_kernel(x_ref, nw_ref, w_ref, o_ref, h_ref):
    @pl.when(pl.program_id(1) == 0)
    def _():
        x = x_ref[...]
        ms = jnp.mean(x * x, axis=-1, keepdims=True)
        h_ref[...] = (x * lax.rsqrt(ms + EPS) * nw_ref[...]).astype(BF16)

    o_ref[...] = _mm(h_ref[...], w_ref[...])


def _norm_proj(x, nw, w, tm, tn):
    m, d = x.shape
    n = w.shape[1]
    return pl.pallas_call(
        _norm_proj_kernel,
        grid=(m // tm, n // tn),
        in_specs=[pl.BlockSpec((tm, d), lambda i, j: (i, 0)),
                  pl.BlockSpec((1, d), lambda i, j: (0, 0)),
                  pl.BlockSpec((d, tn), lambda i, j: (0, j))],
        out_specs=pl.BlockSpec((tm, tn), lambda i, j: (i, j)),
        out_shape=jax.ShapeDtypeStruct((m, n), F32),
        scratch_shapes=[pltpu.VMEM((tm, d), BF16)],
        compiler_params=_cparams(("parallel", "arbitrary")),
        name="norm_proj",
    )(x, nw, w)


def _final_norm_kernel(x_ref, nw_ref, o_ref):
    x = x_ref[...]
    ms = jnp.mean(x * x, axis=-1, keepdims=True)
    o_ref[...] = x * lax.rsqrt(ms + EPS) * nw_ref[...]


def _final_norm(x, nw, tm):
    m, d = x.shape
    return pl.pallas_call(
        _final_norm_kernel,
        grid=(m // tm,),
        in_specs=[pl.BlockSpec((tm, d), lambda i: (i, 0)),
                  pl.BlockSpec((1, d), lambda i: (0, 0))],
        out_specs=pl.BlockSpec((tm, d), lambda i: (i, 0)),
        out_shape=jax.ShapeDtypeStruct((m, d), F32),
        compiler_params=_cparams(("parallel",)),
        name="final_norm",
    )(x, nw)


def _mem_kv_kernel(x_ref, w_ref, k_ref, v_ref):
    r = _mm(x_ref[...].astype(BF16), w_ref[...])
    k_ref[...] = r[:, :WIDTH_MEM]
    v_ref[...] = r[:, WIDTH_MEM:]


def _mem_kv(mem, w):
    m, d = mem.shape
    depth = w.shape[0]
    out = jax.ShapeDtypeStruct((depth, m, WIDTH_MEM), F32)
    return pl.pallas_call(
        _mem_kv_kernel,
        grid=(depth,),
        in_specs=[pl.BlockSpec((m, d), lambda i: (0, 0)),
                  pl.BlockSpec((None, d, 2 * WIDTH_MEM), lambda i: (i, 0, 0))],
        out_specs=[pl.BlockSpec((None, m, WIDTH_MEM), lambda i: (i, 0, 0)),
                   pl.BlockSpec((None, m, WIDTH_MEM), lambda i: (i, 0, 0))],
        out_shape=[out, out],
        compiler_params=_cparams(("parallel",)),
        name="mem_kv",
    )(mem, w)


def _rel_bucket_np(dist):
    dist = np.asarray(dist, np.int64)
    df = np.maximum(dist, 1).astype(np.float32)
    large = MAX_EXACT + (np.log(df / np.float32(MAX_EXACT)) / np.float32(math.log(MAX_DIST / MAX_EXACT))
                         * np.float32(N_BUCKETS - MAX_EXACT)).astype(np.int32)
    return np.where(dist < MAX_EXACT, dist, np.minimum(large, N_BUCKETS - 1)).astype(np.int32)


def _bias_prompt_kernel(bkt_ref, tab_ref, o_ref):
    g = pl.program_id(0)
    bkt = bkt_ref[...]
    accs = [jnp.zeros(bkt.shape, F32) for _ in range(HEADS_PER_GROUP)]
    for b in range(N_BUCKETS):
        hit = bkt == b
        for h in range(HEADS_PER_GROUP):
            accs[h] = jnp.where(hit, tab_ref[b, g * HEADS_PER_GROUP + h], accs[h])
    for h in range(HEADS_PER_GROUP):
        o_ref[h] = accs[h]


def _bias_prompt(rel_bias):
    i = np.arange(BAND)[:, None]
    j = np.arange(2 * BAND)[None, :]
    dsub = np.maximum(i + BAND - j, 0)
    bkt = np.stack([_rel_bucket_np(dsub * dil) for _, dil in DIL_GROUPS])
    return pl.pallas_call(
        _bias_prompt_kernel,
        grid=(N_GROUPS,),
        in_specs=[pl.BlockSpec((None, BAND, 2 * BAND), lambda g: (g, 0, 0)),
                  pl.BlockSpec(memory_space=pltpu.SMEM)],
        out_specs=pl.BlockSpec((None, HEADS_PER_GROUP, BAND, 2 * BAND), lambda g: (g, 0, 0, 0)),
        out_shape=jax.ShapeDtypeStruct((N_GROUPS, HEADS_PER_GROUP, BAND, 2 * BAND), F32),
        compiler_params=_cparams(("arbitrary",)),
        name="bias_prompt",
    )(jnp.asarray(bkt), rel_bias)


def _bias_sample_kernel(bkt_ref, tab_ref, o_ref, *, n_new):
    g = pl.program_id(0)
    H = HEADS_PER_GROUP
    bkt = bkt_ref[...]
    for h in range(H):
        c = g * H + h
        row = jnp.zeros(bkt.shape, F32)
        for b in range(N_BUCKETS):
            row = jnp.where(bkt == b, tab_ref[b, c], row)
        own = jnp.zeros(bkt.shape, F32) + tab_ref[0, c]
        for t in range(n_new):
            o_ref[0, t * H + h: t * H + h + 1, :] = row
            o_ref[1, t * H + h: t * H + h + 1, :] = own


def _bias_sample(rel_bias, n_new):
    rows = n_new * HEADS_PER_GROUP
    bkt = np.stack([_rel_bucket_np((BAND - np.arange(BAND)) * dil) for _, dil in DIL_GROUPS])[:, None, :]
    return pl.pallas_call(
        functools.partial(_bias_sample_kernel, n_new=n_new),
        grid=(N_GROUPS,),
        in_specs=[pl.BlockSpec((None, 1, BAND), lambda g: (g, 0, 0)),
                  pl.BlockSpec(memory_space=pltpu.SMEM)],
        out_specs=pl.BlockSpec((None, 2, rows, BAND), lambda g: (g, 0, 0, 0)),
        out_shape=jax.ShapeDtypeStruct((N_GROUPS, 2, rows, BAND), F32),
        compiler_params=_cparams(("arbitrary",)),
        name="bias_sample",
    )(jnp.asarray(bkt), rel_bias)


def _prompt_attn_kernel(*refs):
    ins, (o_ref, acc_ref, ml_ref) = refs[:-3], refs[-3:]
    first_tile = pl.program_id(1) == 0
    row = lax.broadcasted_iota(jnp.int32, (BAND, BAND), 0)
    col = lax.broadcasted_iota(jnp.int32, (BAND, BAND), 1)
    keep_cur = col <= row
    keep_prev = col >= row
    keep_prev_first = keep_prev & jnp.logical_not(first_tile)

    blocks = [(g, n, r) for g, (_, dil) in enumerate(DIL_GROUPS)
              for n in range(ATTN_TILE // (BAND * dil)) for r in range(dil)]
    for first in range(0, len(blocks), ATTN_BLOCKS_IN_FLIGHT):
        staged = []
        for g, n, r in blocks[first: first + ATTN_BLOCKS_IN_FLIGHT]:
            dil = DIL_GROUPS[g][1]
            q_ref, kc_ref, vc_ref, kp_ref, vp_ref, bias_ref = ins[6 * g: 6 * g + 6]
            cur = pl.ds(n * BAND * dil + r, BAND, stride=dil)
            q = q_ref[cur, :].astype(BF16)
            kc = kc_ref[cur, :]
            if n == 0:
                prev_src, prev, mask_prev = (kp_ref, vp_ref), pl.ds(r, BAND, stride=dil), keep_prev_first
            else:
                prev_src, prev, mask_prev = (kc_ref, vc_ref), pl.ds((n - 1) * BAND * dil + r, BAND, stride=dil), keep_prev
            s_c = jnp.where(keep_cur, _dot_nt(q, kc) * SCALE + bias_ref[:, BAND:2 * BAND], NEG)
            s_p = jnp.where(mask_prev, _dot_nt(q, prev_src[0][prev, :]) * SCALE + bias_ref[:, 0:BAND], NEG)
            staged.append((g, cur, vc_ref, prev_src[1], prev, s_c, s_p))
        probs = []
        for g, cur, vc_ref, vp_src, prev, s_c, s_p in staged:
            m = jnp.maximum(jnp.max(s_c, axis=-1, keepdims=True), jnp.max(s_p, axis=-1, keepdims=True))
            p_c = jnp.exp(s_c - m)
            p_p = jnp.exp(s_p - m)
            l = jnp.sum(p_c, axis=-1, keepdims=True) + jnp.sum(p_p, axis=-1, keepdims=True)
            probs.append((p_c, p_p, m, l))
        for (g, cur, vc_ref, vp_src, prev, _, _), (p_c, p_p, m, l) in zip(staged, probs):
            acc_ref[g, cur, :] = _dot(p_c, vc_ref[cur, :]) + _dot(p_p, vp_src[prev, :])
            ml_ref[g, cur, :] = m
            ml_ref[N_GROUPS + g, cur, :] = l

    chunk = 256
    for c in range(ATTN_TILE // chunk):
        rows = slice(c * chunk, (c + 1) * chunk)
        ms = [ml_ref[g, rows, :] for g in range(N_GROUPS)]
        m = functools.reduce(jnp.maximum, ms)
        num = jnp.zeros((chunk, HEAD_DIM), F32)
        den = jnp.zeros((chunk, 1), F32)
        for g in range(N_GROUPS):
            w = jnp.exp(ms[g] - m)
            num = num + w * acc_ref[g, rows, :]
            den = den + w * ml_ref[N_GROUPS + g, rows, :]
        o_ref[rows, :] = (num / den).astype(o_ref.dtype)


def _prompt_attn(proj, bias_p, batch, seq):
    H = HEADS_PER_GROUP
    assert seq % ATTN_TILE == 0
    ntile = seq // ATTN_TILE
    args, specs = [], []
    for g, (win, dil) in enumerate(DIL_GROUPS):
        assert win // dil == BAND
        span = BAND * dil
        per = ATTN_TILE // span
        qc, kc, vc = g * H, (N_GROUPS + g) * H, (2 * N_GROUPS + g) * H
        cur = lambda c: (lambda b, i, h: (b * ntile + i, c + h))
        prev = lambda c, per=per: (lambda b, i, h: (jnp.maximum((b * ntile + i) * per - 1, 0), c + h))
        specs += [pl.BlockSpec((ATTN_TILE, HEAD_DIM), cur(qc)),
                  pl.BlockSpec((ATTN_TILE, HEAD_DIM), cur(kc)),
                  pl.BlockSpec((ATTN_TILE, HEAD_DIM), cur(vc)),
                  pl.BlockSpec((span, HEAD_DIM), prev(kc)),
                  pl.BlockSpec((span, HEAD_DIM), prev(vc)),
                  pl.BlockSpec((None, None, BAND, 2 * BAND), lambda b, i, h, g=g: (g, h, 0, 0))]
        args += [proj] * 5 + [bias_p]
    return pl.pallas_call(
        _prompt_attn_kernel,
        grid=(batch, ntile, H),
        in_specs=specs,
        out_specs=pl.BlockSpec((ATTN_TILE, HEAD_DIM), lambda b, i, h: (b * ntile + i, h)),
        out_shape=jax.ShapeDtypeStruct((batch * seq, WIDTH_A_OUT), BF16),
        scratch_shapes=[pltpu.VMEM((N_GROUPS, ATTN_TILE, HEAD_DIM), F32),
                        pltpu.VMEM((2 * N_GROUPS, ATTN_TILE, 1), F32)],
        compiler_params=_cparams(("parallel", "arbitrary", "arbitrary")),
        name="prompt_attn",
    )(*args)


def _export_win_kernel(*refs, n_alias):
    k_ref, v_ref = refs[0:2]
    ok_ref, ov_ref = refs[2 + n_alias: 4 + n_alias]
    H = HEADS_PER_GROUP
    rows = k_ref.shape[0]
    for src, dst in ((k_ref, ok_ref), (v_ref, ov_ref)):
        for h in range(H):
            dst[pl.ds(h, rows, stride=H), :] = src[:, h * HEAD_DIM:(h + 1) * HEAD_DIM]


def _export_win(proj, g, layer, n_layers, prev_out, batch, seq):
    H = HEADS_PER_GROUP
    keep = min(DIL_GROUPS[g][0], seq)
    tile = min(keep, 1024)
    assert keep % tile == 0 and seq % tile == 0
    steps = keep // tile
    first = (seq - keep) // tile
    kc = WIDTH_A_QKV // WIDTH_A_OUT + g
    vc = 2 * WIDTH_A_QKV // WIDTH_A_OUT + g
    ospec = pl.BlockSpec((None, None, tile * H, HEAD_DIM), lambda b, i: (layer, b, i, 0))
    alias_args = list(prev_out) if prev_out is not None else []
    n_alias = len(alias_args)
    shape = jax.ShapeDtypeStruct((n_layers, batch, keep * H, HEAD_DIM), F32)
    return pl.pallas_call(
        functools.partial(_export_win_kernel, n_alias=n_alias),
        grid=(batch, steps),
        in_specs=[pl.BlockSpec((tile, WIDTH_A_OUT), lambda b, i: (b * (seq // tile) + first + i, kc)),
                  pl.BlockSpec((tile, WIDTH_A_OUT), lambda b, i: (b * (seq // tile) + first + i, vc))]
                 + [pl.BlockSpec(memory_space=pl.ANY)] * n_alias,
        out_specs=[ospec, ospec],
        out_shape=[shape, shape],
        input_output_aliases={2 + k: k for k in range(n_alias)},
        compiler_params=_cparams(("parallel", "parallel")),
        name=f"export_win_g{g}",
    )(proj, proj, *alias_args)


def _mem_attn_kernel(q_ref, k_ref, v_ref, o_ref):
    heads = [slice(h * HEAD_DIM, (h + 1) * HEAD_DIM) for h in range(MEM_HEADS)]
    scores = [_dot_nt(q_ref[:, hs], k_ref[:, hs]) * SCALE for hs in heads]
    probs = []
    for s in scores:
        p = jnp.exp(s - jnp.max(s, axis=-1, keepdims=True))
        probs.append((p, jnp.sum(p, axis=-1, keepdims=True)))
    for hs, (p, l) in zip(heads, probs):
        o_ref[:, hs] = (_dot(p, v_ref[:, hs]) / l).astype(o_ref.dtype)


def _mem_attn(proj, qcol, mk, mv, layer, batch, seq, tm):
    steps = seq // tm
    return pl.pallas_call(
        _mem_attn_kernel,
        grid=(batch, steps),
        in_specs=[pl.BlockSpec((tm, WIDTH_MEM), lambda b, i: (b * steps + i, qcol)),
                  pl.BlockSpec((None, N_MEM, WIDTH_MEM), lambda b, i: (layer, b, 0)),
                  pl.BlockSpec((None, N_MEM, WIDTH_MEM), lambda b, i: (layer, b, 0))],
        out_specs=pl.BlockSpec((tm, WIDTH_MEM), lambda b, i: (b * steps + i, 0)),
        out_shape=jax.ShapeDtypeStruct((batch * seq, WIDTH_MEM), BF16),
        compiler_params=_cparams(("parallel", "parallel")),
        name="mem_attn",
    )(proj, mk, mv)


def _out_proj_kernel(a1_ref, a2_ref, z1_ref, z2_ref, x_ref, w1_ref, w2_ref, o_ref):
    g1 = (a1_ref[...] * _silu(z1_ref[...])).astype(BF16)
    g2 = (a2_ref[...] * _silu(z2_ref[...])).astype(BF16)
    y = _mm(g1, w1_ref[...])
    y = y + _mm(g2, w2_ref[...])
    o_ref[...] = x_ref[...] + y


def _out_proj(a1, a2, proj, z1col, z2col, x, w1, w2, tm, in_place):
    m, d = x.shape
    k1, k2 = a1.shape[1], a2.shape[1]
    return pl.pallas_call(
        _out_proj_kernel,
        grid=(m // tm,),
        in_specs=[pl.BlockSpec((tm, k1), lambda i: (i, 0)),
                  pl.BlockSpec((tm, k2), lambda i: (i, 0)),
                  pl.BlockSpec((tm, k1), lambda i: (i, z1col)),
                  pl.BlockSpec((tm, k2), lambda i: (i, z2col)),
                  pl.BlockSpec((tm, d), lambda i: (i, 0)),
                  pl.BlockSpec((k1, d), lambda i: (0, 0)),
                  pl.BlockSpec((k2, d), lambda i: (0, 0))],
        out_specs=pl.BlockSpec((tm, d), lambda i: (i, 0)),
        out_shape=jax.ShapeDtypeStruct((m, d), F32),
        input_output_aliases={4: 0} if in_place else {},
        compiler_params=_cparams(("parallel",)),
        name="out_proj",
    )(a1, a2, proj, proj, x, w1, w2)


def _new_rows(new_ref, base, g, n_new):
    H = HEADS_PER_GROUP
    return jnp.concatenate([new_ref[t, base + g * H: base + (g + 1) * H, :] for t in range(n_new)], axis=0)


def _cache_update_copies(new_ref, cin, cout_hbm, rows_ref, sems, layer, seq_idx, n_new):
    H = HEADS_PER_GROUP
    R = n_new * H
    copies = []
    for g in range(N_GROUPS):
        for kv, base in ((0, N_GROUPS * H), (1, 2 * N_GROUPS * H)):
            a = 2 * g + kv
            src, dst = cin[g][kv], cout_hbm[g][kv]
            n = src.shape[2]
            rows_ref[a] = _new_rows(new_ref, base, g, n_new)
            copies.append(pltpu.make_async_copy(src.at[0, 0, pl.ds(R, n - R), :],
                                                dst.at[layer, seq_idx, pl.ds(0, n - R), :], sems.at[2 * a]))
            copies.append(pltpu.make_async_copy(rows_ref.at[a],
                                                dst.at[layer, seq_idx, pl.ds(n - R, R), :], sems.at[2 * a + 1]))
    return copies


def _sample_win_body(new_ref, cin, bias_ref, o_ref, kcat_ref, vcat_ref, n_new):
    H = HEADS_PER_GROUP
    R = n_new * H
    NPART = 4
    qb, kb, vb = 0, N_GROUPS * H, 2 * N_GROUPS * H

    rows0 = cin[0][0].shape[0]
    kcat_ref[0:rows0, :] = cin[0][0][...]
    vcat_ref[0:rows0, :] = cin[0][1][...]
    kcat_ref[rows0:rows0 + R, :] = _new_rows(new_ref, kb, 0, n_new)
    vcat_ref[rows0:rows0 + R, :] = _new_rows(new_ref, vb, 0, n_new)

    lane = lax.broadcasted_iota(jnp.int32, (R, BAND), 1)
    src = [(kcat_ref, vcat_ref) if g == 0 else cin[g] for g in range(N_GROUPS)]
    softmaxed = []
    for g, (win, dil) in enumerate(DIL_GROUPS):
        q = _new_rows(new_ref, qb, g, n_new)
        pitch = dil * H
        sc = [jnp.zeros((R, BAND), F32) for _ in range(NPART)]
        for s in range(BAND):
            red = jnp.sum(src[g][0][s * pitch: s * pitch + R, :] * q, axis=-1, keepdims=True)
            sc[s % NPART] = jnp.where(lane == s, red, sc[s % NPART])
        scores = ((sc[0] + sc[1]) + (sc[2] + sc[3])) * SCALE + bias_ref[g, 0]
        s_own = jnp.sum(_new_rows(new_ref, kb, g, n_new) * q, axis=-1, keepdims=True) * SCALE \
            + bias_ref[g, 1][:, 0:1]
        m = jnp.maximum(jnp.max(scores, axis=-1, keepdims=True), s_own)
        p = jnp.exp(scores - m)
        p_own = jnp.exp(s_own - m)
        softmaxed.append((p, p_own, m, jnp.sum(p, axis=-1, keepdims=True) + p_own))
    parts = []
    for g, (win, dil) in enumerate(DIL_GROUPS):
        p, p_own, m, l = softmaxed[g]
        pitch = dil * H
        acc = [jnp.zeros((R, HEAD_DIM), F32) for _ in range(NPART)]
        for s in range(BAND):
            acc[s % NPART] = acc[s % NPART] + p[:, s:s + 1] * src[g][1][s * pitch: s * pitch + R, :]
        parts.append((((acc[0] + acc[1]) + (acc[2] + acc[3])) + p_own * _new_rows(new_ref, vb, g, n_new), m, l))

    m = functools.reduce(jnp.maximum, [pt[1] for pt in parts])
    num = jnp.zeros((R, HEAD_DIM), F32)
    den = jnp.zeros((R, 1), F32)
    for acc, mg, lg in parts:
        w = jnp.exp(mg - m)
        num = num + w * acc
        den = den + w * lg
    o = num / den
    for t in range(n_new):
        o_ref[t] = o[t * H:(t + 1) * H, :]


def _sample_mem_body(q_ref, qblk, k_ref, v_ref, o_ref, n_new):
    pad = jnp.zeros((8 - n_new, HEAD_DIM), F32)
    scores = []
    for h in range(MEM_HEADS):
        q = jnp.concatenate([q_ref[t, qblk + h: qblk + h + 1, :] for t in range(n_new)] + [pad], axis=0)
        scores.append(_dot_nt(q, k_ref[pl.ds(h, N_MEM, stride=MEM_HEADS), :]) * SCALE)
    probs = []
    for s in scores:
        p = jnp.exp(s - jnp.max(s, axis=-1, keepdims=True))
        probs.append((p, jnp.sum(p, axis=-1, keepdims=True)))
    for h, (p, l) in enumerate(probs):
        o = _dot(p, v_ref[pl.ds(h, N_MEM, stride=MEM_HEADS), :]) / l
        for t in range(n_new):
            o_ref[t, h:h + 1, :] = o[t:t + 1, :]


def _gate_terms(ba, alog_row, dtb_row):
    beta = _sigmoid(ba)
    g = -jnp.exp(alog_row) * _softplus(ba + dtb_row)
    return beta, g


def _delta_prompt_body(x_ref, halo_ref, ba_ref, cw_ref, alog_ref, dtb_ref, onw_ref,
                       o_ref, xs_ref, s_ref):
    C = BLOCK_ROWS
    NH = N_HEADS_B
    it = pl.program_id(1)

    @pl.when(it == 0)
    def _():
        s_ref[...] = jnp.zeros(s_ref.shape, F32)
        xs_ref[0:8, :] = jnp.zeros((8, xs_ref.shape[1]), F32)

    @pl.when(it > 0)
    def _():
        xs_ref[0:8, :] = halo_ref[...]

    xs_ref[8:8 + C, :] = x_ref[...]

    def conv_block(c0):
        cs = slice(c0, c0 + HEAD_DIM)
        acc = xs_ref[pl.ds(8 - (CONV_W - 1), C), cs] * cw_ref[0:1, cs]
        for wi in range(1, CONV_W):
            acc = acc + xs_ref[pl.ds(8 - (CONV_W - 1) + wi, C), cs] * cw_ref[wi:wi + 1, cs]
        return _silu(acc)

    beta, g = _gate_terms(ba_ref[...], alog_ref[...], dtb_ref[...])
    row = lax.broadcasted_iota(jnp.int32, (C, C), 0)
    col = lax.broadcasted_iota(jnp.int32, (C, C), 1)
    tri = row >= col
    strict = row > col
    ones_tri = jnp.where(tri, 1.0, 0.0).astype(BF16)
    gh, gm, gl = _split3(g)
    gc = _mm(ones_tri, gh) + (_mm(ones_tri, gm) + _mm(ones_tri, gl))
    gc_t = gc.T
    eye = jnp.where(row == col, 1.0, 0.0)

    q, k, v, kb, egc, decay, b_col, gc_col = [], [], [], [], [], [], [], []
    m_hi, m_lo, tinv = [], [], []
    for h in range(NH):
        q.append(_l2norm(conv_block(h * HEAD_DIM)) * SCALE)
        k.append(_l2norm(conv_block(WIDTH_B + h * HEAD_DIM)))
        v.append(conv_block(2 * WIDTH_B + h * HEAD_DIM))
        b_col.append(beta[:, h:h + 1])
        gc_col.append(gc[:, 8 + h: 9 + h])
        gc_row = gc_t[8 + h: 9 + h, :]
        decay.append(jnp.where(tri, jnp.exp(jnp.where(tri, gc_col[h] - gc_row, 0.0)), 0.0))
        kb.append(k[h] * b_col[h])
        egc.append(jnp.exp(gc_col[h]))
        lmat = jnp.where(strict, _dot_nt(kb[h], k[h]) * decay[h], 0.0)
        hi, lo = _split2(-lmat)
        m_hi.append(hi)
        m_lo.append(lo)
        tinv.append(eye - lmat)

    for _ in range(int(math.log2(C)) - 1):
        for h in range(NH):
            sq = _mm(m_hi[h], m_hi[h]) + (_mm(m_hi[h], m_lo[h]) + _mm(m_lo[h], m_hi[h]))
            m_hi[h], m_lo[h] = _split2(sq)
        for h in range(NH):
            t_hi, t_lo = _split2(tinv[h])
            tinv[h] = tinv[h] + (_mm(t_hi, m_hi[h]) + (_mm(t_hi, m_lo[h]) + _mm(t_lo, m_hi[h])))

    uw = [_dot(tinv[h], jnp.concatenate([v[h] * b_col[h], kb[h] * egc[h]], axis=1)) for h in range(NH)]
    aqk = [_dot_nt(q[h], k[h]) * decay[h] for h in range(NH)]
    for h in range(NH):
        hs = slice(h * HEAD_DIM, (h + 1) * HEAD_DIM)
        g_last = gc_col[h][C - 1:C, :]
        kdec = k[h] * jnp.exp(g_last - gc_col[h])
        s_prev = s_ref[h]
        rs = _dot(jnp.concatenate([uw[h][:, HEAD_DIM:], q[h] * egc[h]], axis=0), s_prev)
        v_new = uw[h][:, :HEAD_DIM] - rs[0:C]
        o = rs[C:2 * C] + _dot(aqk[h], v_new)
        s_ref[h] = s_prev * jnp.exp(g_last) + _dot(kdec.T, v_new)
        ms = jnp.mean(o * o, axis=-1, keepdims=True)
        o_ref[:, hs] = (o * lax.rsqrt(ms + EPS) * onw_ref[...]).astype(o_ref.dtype)


N_DP_IN = 7


def _prompt_delta_sample_win_kernel(*refs, n_new, n_alias, win_layer):
    dp_in = refs[:N_DP_IN]
    new_ref = refs[N_DP_IN]
    c = refs[N_DP_IN + 1: N_DP_IN + 7]
    bias_ref, memk_ref, memv_ref = refs[N_DP_IN + 7: N_DP_IN + 10]
    outs = refs[N_DP_IN + 10 + n_alias:]
    o_ref, s_out_ref = outs[0:2]
    co = outs[2:8]
    os_ref, om_ref = outs[8:10]
    xs_ref, s_ref, kcat_ref, vcat_ref, rows_ref, sems = outs[10:16]
    pairs = lambda r: ((r[0], r[1]), (r[2], r[3]), (r[4], r[5]))
    cin = pairs([x.at[0, 0] for x in c])
    seq_idx = pl.program_id(0) * pl.num_programs(1) + pl.program_id(1)
    copies = _cache_update_copies(new_ref, pairs(c), pairs(co), rows_ref, sems, win_layer, seq_idx, n_new)
    for cp in copies:
        cp.start()
    _delta_prompt_body(*dp_in, o_ref, xs_ref, s_ref)
    _sample_win_body(new_ref, cin, bias_ref, os_ref, kcat_ref, vcat_ref, n_new)
    _sample_mem_body(new_ref, 4 * 9, memk_ref, memv_ref, om_ref, n_new)
    for cp in copies:
        cp.wait()

    @pl.when(pl.program_id(1) == pl.num_programs(1) - 1)
    def _():
        s_out_ref[...] = s_ref[...]


def _prompt_delta_sample_win(proj_p, conv_w, alog_row, dtb_row, onw, bp, seq,
                             proj_s, caches, win_layer, prev_out, bias_s, mem_k, mem_v, mem_layer, bs, n_new):
    C = BLOCK_ROWS
    H = HEADS_PER_GROUP
    steps = seq // C
    assert bs == bp * steps and n_new * H % 8 == 0
    for g, (win, dil) in enumerate(DIL_GROUPS):
        assert caches[2 * g].shape[2] == win * H and win // dil == BAND
    wq = 3 * WIDTH_B
    sb = lambda b, i: b * steps + i
    const2 = lambda shape: pl.BlockSpec(shape, lambda b, i: (0, 0))
    new = proj_s.reshape(bs, n_new, IN_A // 128, 128)
    cspecs = [pl.BlockSpec((1, 1) + c.shape[2:], lambda b, i: (win_layer, sb(b, i), 0, 0)) for c in caches]
    mspec = pl.BlockSpec((None, None, N_MEM * MEM_HEADS, HEAD_DIM), lambda b, i: (mem_layer, sb(b, i), 0, 0))
    rows0 = caches[0].shape[2]
    alias_args = list(prev_out) if prev_out is not None else []
    n_alias = len(alias_args)
    n_in = N_DP_IN + 10
    small = lambda w: pl.BlockSpec((None, n_new, H, w), lambda b, i: (sb(b, i), 0, 0, 0))
    hbm = pl.BlockSpec(memory_space=pl.ANY)
    outs = pl.pallas_call(
        functools.partial(_prompt_delta_sample_win_kernel, n_new=n_new, n_alias=n_alias, win_layer=win_layer),
        grid=(bp, steps),
        in_specs=[pl.BlockSpec((C, wq), lambda b, i: (sb(b, i), 0)),
                  pl.BlockSpec((8, wq), lambda b, i: (jnp.maximum(sb(b, i) * (C // 8) - 1, 0), 0)),
                  pl.BlockSpec((C, 128), lambda b, i: (sb(b, i), BA_COL_BLOCK)),
                  const2((CONV_W, wq)), const2((1, 128)), const2((1, 128)), const2((1, HEAD_DIM)),
                  pl.BlockSpec((None, n_new, IN_A // 128, 128), lambda b, i: (sb(b, i), 0, 0, 0))]
                 + cspecs
                 + [pl.BlockSpec(bias_s.shape, lambda b, i: (0, 0, 0, 0)), mspec, mspec]
                 + [hbm] * n_alias,
        out_specs=[pl.BlockSpec((C, WIDTH_B), lambda b, i: (sb(b, i), 0)),
                   pl.BlockSpec((None, N_HEADS_B, HEAD_DIM, HEAD_DIM), lambda b, i: (b, 0, 0, 0))]
                  + [hbm] * len(caches) + [small(HEAD_DIM), small(HEAD_DIM)],
        out_shape=[jax.ShapeDtypeStruct((bp * seq, WIDTH_B), BF16),
                   jax.ShapeDtypeStruct((bp, N_HEADS_B, HEAD_DIM, HEAD_DIM), F32)]
                  + [jax.ShapeDtypeStruct(c.shape, F32) for c in caches]
                  + [jax.ShapeDtypeStruct((bs, n_new, H, HEAD_DIM), F32),
                     jax.ShapeDtypeStruct((bs, n_new, MEM_HEADS, HEAD_DIM), F32)],
        scratch_shapes=[pltpu.VMEM((8 + C, wq), F32),
                        pltpu.VMEM((N_HEADS_B, HEAD_DIM, HEAD_DIM), F32),
                        pltpu.VMEM((rows0 + n_new * H, 128), F32),
                        pltpu.VMEM((rows0 + n_new * H, 128), F32),
                        pltpu.VMEM((len(caches), n_new * H, HEAD_DIM), F32),
                        pltpu.SemaphoreType.DMA((2 * len(caches),))],
        input_output_aliases={n_in + k: 2 + k for k in range(n_alias)},
        compiler_params=_cparams(("parallel", "arbitrary")),
        name="prompt_delta_sample_win",
    )(proj_p, proj_p, proj_p, conv_w, alog_row, dtb_row, onw, new, *caches, bias_s, mem_k, mem_v, *alias_args)
    o_p, s_p = outs[0], outs[1]
    return (o_p, s_p, outs[2:8], outs[8].reshape(bs * n_new, WIDTH_A_OUT), outs[9].reshape(bs * n_new, WIDTH_MEM))


DELTA_SAMPLE_SEQS = 2


def _delta_sample_kernel(*refs, n_new, n_alias):
    x_ref, cs_ref, s_in_ref, cw_ref, alog_ref, dtb_ref, onw_ref, memk_ref, memv_ref = refs[:9]
    o_ref, s_out_ref, om_ref, st_ref, rs_ref, kd_ref, vn_ref = refs[9 + n_alias:]
    for s in range(DELTA_SAMPLE_SEQS):
        _delta_sample_body(x_ref.at[s], cs_ref.at[0, s], s_in_ref.at[0, s], cw_ref, alog_ref, dtb_ref, onw_ref,
                           memk_ref.at[0, s], memv_ref.at[0, s], o_ref.at[s], s_out_ref.at[0, s], om_ref.at[s],
                           st_ref.at[s], rs_ref.at[s], kd_ref.at[s], vn_ref.at[s], n_new)


def _delta_sample_body(x_ref, cs_ref, s_in_ref, cw_ref, alog_ref, dtb_ref, onw_ref, memk_ref, memv_ref,
                       o_ref, s_out_ref, om_ref, st_ref, rs_ref, kd_ref, vn_ref, n_new):
    NH = N_HEADS_B
    qkv_rows = 3 * NH

    _sample_mem_body(x_ref, 4 * 9, memk_ref, memv_ref, om_ref, n_new)

    xp = [cs_ref[i] for i in range(CONV_W - 1)] + [x_ref[t, 0:qkv_rows, :] for t in range(n_new)]
    ba = jnp.concatenate([x_ref[t, BA_COL_BLOCK:BA_COL_BLOCK + 1, :] for t in range(n_new)]
                         + [jnp.zeros((HEAD_DIM - n_new, 128), F32)], axis=0)
    ba_t = ba.T
    alog_col = alog_ref[:, 0:1]
    dtb_col = dtb_ref[:, 0:1]

    q, k, v, beta, gc = [], [], [], [], []
    run = jnp.zeros((NH, 1), F32)
    for t in range(n_new):
        acc = xp[t] * cw_ref[0]
        for wi in range(1, CONV_W):
            acc = acc + xp[t + wi] * cw_ref[wi]
        conv = _silu(acc)
        q.append(_l2norm(conv[0:NH]) * SCALE)
        k.append(_l2norm(conv[NH:2 * NH]))
        v.append(conv[2 * NH:3 * NH])
        beta.append(_sigmoid(ba_t[0:NH, t:t + 1]))
        run = run - jnp.exp(alog_col) * _softplus(ba_t[NH:2 * NH, t:t + 1] + dtb_col)
        gc.append(run)
    egc = [jnp.exp(x) for x in gc]
    dec = {(i, j): jnp.exp(gc[i] - gc[j]) for i in range(n_new) for j in range(i)}
    lsum = lambda a, b: jnp.sum(a * b, axis=-1, keepdims=True)

    u, w = [], []
    for i in range(n_new):
        ui = v[i] * beta[i]
        wi_ = k[i] * (beta[i] * egc[i])
        for j in range(i):
            lij = beta[i] * lsum(k[i], k[j]) * dec[i, j]
            ui = ui - lij * u[j]
            wi_ = wi_ - lij * w[j]
        u.append(ui)
        w.append(wi_)

    for i in range(n_new):
        st_ref[i * NH:(i + 1) * NH, :] = w[i]
        st_ref[(n_new + i) * NH:(n_new + i + 1) * NH, :] = q[i] * egc[i]
    for h in range(NH):
        rs_ref[h * 8:(h + 1) * 8, :] = _dot(st_ref[pl.ds(h, 2 * n_new, stride=NH), :], s_in_ref[h])

    vn, g_last = [], gc[n_new - 1]
    for i in range(n_new):
        vn.append(u[i] - rs_ref[pl.ds(i, NH, stride=8), :])
    for i in range(n_new):
        o = rs_ref[pl.ds(n_new + i, NH, stride=8), :] + lsum(q[i], k[i]) * vn[i]
        for j in range(i):
            o = o + (lsum(q[i], k[j]) * dec[i, j]) * vn[j]
        ms = jnp.mean(o * o, axis=-1, keepdims=True)
        o_ref[i] = o * lax.rsqrt(ms + EPS) * onw_ref[...]

    kd_ref[...] = jnp.zeros(kd_ref.shape, F32)
    vn_ref[...] = jnp.zeros(vn_ref.shape, F32)
    for i in range(n_new):
        kd_ref[i * NH:(i + 1) * NH, :] = k[i] * jnp.exp(g_last - gc[i])
        vn_ref[i * NH:(i + 1) * NH, :] = vn[i]
    eg_last = jnp.exp(g_last)
    zpad = jnp.zeros((HEAD_DIM - 8, HEAD_DIM), F32)
    for h in range(NH):
        kd = jnp.concatenate([kd_ref[pl.ds(h, 8, stride=NH), :], zpad], axis=0)
        vh = jnp.concatenate([vn_ref[pl.ds(h, 8, stride=NH), :], zpad], axis=0)
        s_out_ref[h] = s_in_ref[h] * eg_last[h:h + 1, :] + _dot(kd.T, vh)


def _delta_sample(proj_s, conv_state, delta_state, layer, prev_out, conv_w, a_log, dt_bias, onw,
                  mem_k, mem_v, mem_layer, batch, n_new):
    assert n_new == 4 and N_HEADS_B == 8
    ncol = IN_B_PAD // 128
    new = proj_s.reshape(batch, n_new, ncol, 128)
    cw = conv_w.reshape(CONV_W, 3 * N_HEADS_B, HEAD_DIM)
    alog = jnp.broadcast_to(a_log[:, None], (N_HEADS_B, 128))
    dtb = jnp.broadcast_to(dt_bias[:, None], (N_HEADS_B, 128))
    S = DELTA_SAMPLE_SEQS
    assert batch % S == 0
    sspec = pl.BlockSpec((1, S, N_HEADS_B, HEAD_DIM, HEAD_DIM), lambda b: (layer, b, 0, 0, 0))
    mspec = pl.BlockSpec((1, S, N_MEM * MEM_HEADS, HEAD_DIM), lambda b: (mem_layer, b, 0, 0))
    alias_args = [prev_out] if prev_out is not None else []
    n_alias = len(alias_args)
    n_in = 9
    o, s, om = pl.pallas_call(
        functools.partial(_delta_sample_kernel, n_new=n_new, n_alias=n_alias),
        grid=(batch // S,),
        in_specs=[pl.BlockSpec((S, n_new, ncol, 128), lambda b: (b, 0, 0, 0)),
                  pl.BlockSpec((1, S, CONV_W - 1, 3 * N_HEADS_B, HEAD_DIM), lambda b: (layer, b, 0, 0, 0)),
                  sspec,
                  pl.BlockSpec(cw.shape, lambda b: (0, 0, 0)),
                  pl.BlockSpec((N_HEADS_B, 128), lambda b: (0, 0)),
                  pl.BlockSpec((N_HEADS_B, 128), lambda b: (0, 0)),
                  pl.BlockSpec((1, HEAD_DIM), lambda b: (0, 0)),
                  mspec, mspec]
                 + [pl.BlockSpec(memory_space=pl.ANY)] * n_alias,
        out_specs=[pl.BlockSpec((S, n_new, N_HEADS_B, HEAD_DIM), lambda b: (b, 0, 0, 0)), sspec,
                   pl.BlockSpec((S, n_new, MEM_HEADS, HEAD_DIM), lambda b: (b, 0, 0, 0))],
        out_shape=[jax.ShapeDtypeStruct((batch, n_new, N_HEADS_B, HEAD_DIM), F32),
                   jax.ShapeDtypeStruct(delta_state.shape, F32),
                   jax.ShapeDtypeStruct((batch, n_new, MEM_HEADS, HEAD_DIM), F32)],
        scratch_shapes=[pltpu.VMEM((S, 2 * n_new * N_HEADS_B, HEAD_DIM), F32),
                        pltpu.VMEM((S, 8 * N_HEADS_B, HEAD_DIM), F32),
                        pltpu.VMEM((S, 8 * N_HEADS_B, HEAD_DIM), F32),
                        pltpu.VMEM((S, 8 * N_HEADS_B, HEAD_DIM), F32)],
        input_output_aliases={n_in + k: 1 + k for k in range(n_alias)},
        compiler_params=_cparams(("parallel",)),
        name="delta_sample",
    )(new, conv_state, delta_state, cw, alog, dtb, onw, mem_k, mem_v, *alias_args)
    return o.reshape(batch * n_new, WIDTH_B), s, om.reshape(batch * n_new, WIDTH_MEM)


def _reorder_w_in_b(w):
    q0 = 3 * WIDTH_B
    b0, a0, m0, z0 = q0, q0 + N_HEADS_B, q0 + 2 * N_HEADS_B, q0 + 2 * N_HEADS_B + WIDTH_MEM
    parts = [w[:, :q0], w[:, z0:], w[:, m0:z0], w[:, b0:a0], w[:, a0:m0]]
    out = jnp.concatenate(parts, axis=1)
    return jnp.pad(out, ((0, 0), (0, IN_B_PAD - out.shape[1])))


def _lane_row(vals, offset):
    return jnp.zeros((1, 128), F32).at[0, offset:offset + vals.shape[0]].set(vals)


def kernel(x_prompt, x_sample, cache_win_k0, cache_win_v0, cache_win_k1, cache_win_v1, cache_win_k2, cache_win_v2, state_conv, state_delta, cache_mem_k, cache_mem_v, mem_prompt, norm_w, final_norm_w, rel_bias, w_in_a, w_out_a, w_in_b, conv_w, a_log, dt_bias, o_norm_w, w_out_b, w_mem_kv):
    bp, seq, d = x_prompt.shape
    bs, n_new, _ = x_sample.shape
    depth = norm_w.shape[0]
    assert depth % 2 == 0
    H = HEADS_PER_GROUP
    tm_p = 1024
    ms = bs * n_new

    xp = x_prompt.reshape(bp * seq, d)
    xs = x_sample.reshape(ms, d)

    mk_all, mv_all = _mem_kv(mem_prompt.reshape(bp * N_MEM, d), w_mem_kv.astype(BF16))
    bias_p = _bias_prompt(rel_bias)
    bias_s = _bias_sample(rel_bias, n_new)
    mem_k_s = cache_mem_k.reshape(depth, bs, N_MEM * MEM_HEADS, HEAD_DIM)
    mem_v_s = cache_mem_v.reshape(depth, bs, N_MEM * MEM_HEADS, HEAD_DIM)
    caches_in = (cache_win_k0, cache_win_v0, cache_win_k1, cache_win_v1, cache_win_k2, cache_win_v2)
    caches_flat = [c.reshape(c.shape[0], bs, c.shape[2] * H, HEAD_DIM) for c in caches_in]
    conv_state = state_conv.reshape(state_conv.shape[0], bs, CONV_W - 1, 3 * N_HEADS_B, HEAD_DIM)

    p_win = [None] * N_GROUPS
    p_conv, p_delta, s_conv = [], [], []
    s_win, s_delta = None, None

    for li in range(depth // 2):
        ia, ib = 2 * li, 2 * li + 1
        w_in = w_in_a[li].astype(BF16)
        w_out_even = w_out_a[li].astype(BF16)
        nw = norm_w[ia].reshape(1, d)
        proj_p = _norm_proj(xp, nw, w_in, 2 * tm_p, 1024)
        proj_s_even = _norm_proj(xs, nw, w_in, ms, 2048)
        for g in range(N_GROUPS):
            p_win[g] = _export_win(proj_p, g, li, depth // 2, p_win[g], bp, seq)
        o_p = _prompt_attn(proj_p, bias_p, bp, seq)
        om_p = _mem_attn(proj_p, 9, mk_all, mv_all, ia, bp, seq, 512)
        xp = _out_proj(o_p, om_p, proj_p, 10, 11, xp, w_out_even[:WIDTH_A_OUT], w_out_even[WIDTH_A_OUT:], 512,
                       in_place=li > 0)

        w_in = _reorder_w_in_b(w_in_b[li]).astype(BF16)
        w_out = w_out_b[li].astype(BF16)
        onw = o_norm_w[li].reshape(1, HEAD_DIM)
        nw = norm_w[ib].reshape(1, d)
        wq = 3 * WIDTH_B
        proj_p = _norm_proj(xp, nw, w_in, tm_p, 1792)
        o_p, s_p, s_win, o_s, om_s = _prompt_delta_sample_win(
            proj_p, conv_w[li], _lane_row(a_log[li], N_HEADS_B), _lane_row(dt_bias[li], N_HEADS_B), onw, bp, seq,
            proj_s_even, caches_flat, li, s_win, bias_s, mem_k_s, mem_v_s, ia, bs, n_new)
        p_delta.append(s_p)
        p_conv.append(proj_p.reshape(bp, seq, IN_B_PAD)[:, seq - (CONV_W - 1):, :wq])
        om_p = _mem_attn(proj_p, 9, mk_all, mv_all, ib, bp, seq, 512)
        xp = _out_proj(o_p, om_p, proj_p, 3, 8, xp, w_out[:WIDTH_B], w_out[WIDTH_B:], 512, in_place=True)
        xs = _out_proj(o_s, om_s, proj_s_even, 10, 11, xs, w_out_even[:WIDTH_A_OUT], w_out_even[WIDTH_A_OUT:], ms,
                       in_place=li > 0)

        proj_s = _norm_proj(xs, nw, w_in, ms, 1792)
        o_s, s_delta, om_s = _delta_sample(proj_s, conv_state, state_delta, li, s_delta, conv_w[li],
                                           a_log[li], dt_bias[li], onw, mem_k_s, mem_v_s, ib, bs, n_new)
        xcat = jnp.concatenate([state_conv[li], proj_s.reshape(bs, n_new, IN_B_PAD)[:, :, :wq]], axis=1)
        s_conv.append(xcat[:, n_new:])
        xs = _out_proj(o_s, om_s, proj_s, 3, 8, xs, w_out[:WIDTH_B], w_out[WIDTH_B:], ms, in_place=True)

    fnw = final_norm_w.reshape(1, d)
    y_prompt = _final_norm(xp, fnw, tm_p).reshape(bp, seq, d)
    y_sample = _final_norm(xs, fnw, ms).reshape(bs, n_new, d)
    p_mk = mk_all.reshape(depth, bp, N_MEM, MEM_HEADS, HEAD_DIM)
    p_mv = mv_all.reshape(depth, bp, N_MEM, MEM_HEADS, HEAD_DIM)
    s_win = [s_win[n].reshape(caches_in[n].shape) for n in range(2 * N_GROUPS)]
    p_win = [w.reshape(w.shape[0], bp, w.shape[2] // H, H, HEAD_DIM) for pair in p_win for w in pair]
    return (y_prompt, y_sample,
            p_win[0], p_win[1], p_win[2], p_win[3], p_win[4], p_win[5],
            jnp.stack(p_conv), jnp.stack(p_delta), p_mk, p_mv,
            s_win[0], s_win[1], s_win[2], s_win[3], s_win[4], s_win[5],
            jnp.stack(s_conv), s_delta)
```
